```python
import math
import jax
import jax.numpy as jnp
from jax import lax
import numpy as np

D_MODEL = 2048
BATCH = 2
SEQ = 16384
DEPTH = 2

MIX_W = 1024
N_BRANCH = 4
DA_HEADS = 8
DA_DK = 64
DA_DV = 2 * DA_DK
ROT_DIM = DA_DK // 4
ROPE_THETA = 500000.0
Q_BLOCK = 128
NEG_INF = -1e30
GM_GROUPS = 8
GM_CHUNK = 128
GM_GC = MIX_W // GM_GROUPS
POOL_WINDOWS = (2, 4, 8, 16)
POOL_GROUPS = 4
POOL_GC = MIX_W // POOL_GROUPS
CONV_W = 3
DA_QK_W = DA_HEADS * 2 * DA_DK
DA_V_W = DA_HEADS * DA_DV
IN_WIDTHS = (DA_QK_W, DA_QK_W, DA_V_W, 2 * MIX_W, MIX_W, 3 * MIX_W)
IN_COLS = sum(IN_WIDTHS)
SPLIT_POINTS = tuple(int(s) for s in np.cumsum(IN_WIDTHS)[:-1])
N_EXPERTS = 32
TOP_K = 4
D_EXPERT = 1024
SWIGLU_LIMIT = 7.0
SWIGLU_ALPHA = 1.702
MOE_BLOCK = 512
EPS = 1e-6

kernel_name = 'hybrid_gated_parallel_mixer_moe'


def rmsnorm(x, g):
    xf = x.astype(jnp.float32)
    y = xf * lax.rsqrt(jnp.mean(xf * xf, axis=-1, keepdims=True) + EPS)
    return (y * g.astype(jnp.float32)).astype(x.dtype)


def layernorm(x, g, b):
    xf = x.astype(jnp.float32)
    mu = jnp.mean(xf, axis=-1, keepdims=True)
    var = jnp.mean(jnp.square(xf - mu), axis=-1, keepdims=True)
    y = (xf - mu) * lax.rsqrt(var + EPS) * g.astype(jnp.float32) + b.astype(jnp.float32)
    return y.astype(x.dtype)


def rotary_tables(positions):
    inv_freq = ROPE_THETA ** (-jnp.arange(0, ROT_DIM, 2, dtype=jnp.float32) / ROT_DIM)
    ang = positions.astype(jnp.float32)[..., None] * inv_freq
    return jnp.cos(ang)[:, :, None, None, :], jnp.sin(ang)[:, :, None, None, :]


def apply_partial_rotary(x, cos, sin):
    half = ROT_DIM // 2
    xf = x.astype(jnp.float32)
    x1 = xf[..., :half]
    x2 = xf[..., half:ROT_DIM]
    out = jnp.concatenate([x1 * cos - x2 * sin, x2 * cos + x1 * sin, xf[..., ROT_DIM:]], axis=-1)
    return out.astype(x.dtype)


def differential_attention(q, k, v, lam):
    bsz, s, h, _, dk = q.shape
    nb = s // Q_BLOCK
    scale = dk ** -0.5
    qb = jnp.moveaxis(q.reshape(bsz, nb, Q_BLOCK, h, 2, dk), 1, 0)
    kpos = jnp.arange(s)

    def block(args):
        qi, bi = args
        sc = jnp.einsum('bqhcd,bkhcd->bhcqk', qi, k).astype(jnp.float32) * scale
        qpos = bi * Q_BLOCK + jnp.arange(Q_BLOCK)
        mask = kpos[None, :] <= qpos[:, None]
        sc = jnp.where(mask, sc, NEG_INF)
        p = jax.nn.softmax(sc, axis=-1)
        a = p[:, :, 0] - lam * p[:, :, 1]
        return jnp.einsum('bhqk,bkhd->bqhd', a.astype(v.dtype), v)

    out = lax.map(block, (qb, jnp.arange(nb)))
    return jnp.moveaxis(out, 0, 1).reshape(bsz, s, h, v.shape[-1])


def spatial_gating(z, ln_g, ln_b, ws, bs):
    bsz, s, _ = z.shape
    z = jax.nn.gelu(z, approximate=False)
    u, v = jnp.split(z, 2, axis=-1)
    v = layernorm(v, ln_g, ln_b)
    vc = v.reshape(bsz, s // GM_CHUNK, GM_CHUNK, GM_GROUPS, GM_GC)
    causal = jnp.tril(jnp.ones((GM_CHUNK, GM_CHUNK), dtype=bool))
    wm = jnp.where(causal[None], ws, jnp.zeros_like(ws))
    sv = jnp.einsum('gts,bnsgc->bntgc', wm, vc) + bs.T[None, None, :, :, None]
    return u * sv.reshape(bsz, s, MIX_W)


def multiscale_pool(z, pw, ps):
    bsz, s, _ = z.shape
    zf = z.astype(jnp.float32).reshape(bsz, s, POOL_GROUPS, POOL_GC)
    cs = jnp.concatenate([jnp.zeros((bsz, 1, POOL_GROUPS, POOL_GC), jnp.float32),
                          jnp.cumsum(zf, axis=1)], axis=1)
    t = jnp.arange(s)
    outs = []
    for g, w in enumerate(POOL_WINDOWS):
        lo = jnp.maximum(t + 1 - w, 0)
        win = cs[:, 1:, g] - cs[:, lo, g]
        cnt = jnp.minimum(t + 1, w).astype(jnp.float32)[None, :, None]
        outs.append(win / cnt - zf[:, :, g])
    pooled = jnp.stack(outs, axis=2).astype(z.dtype)
    mixed = jnp.einsum('bsgc,gcd->bsgd', pooled, pw)
    return mixed.reshape(bsz, s, MIX_W) * ps


def short_conv_mixer(z, cw):
    b_gate, c_gate, hh = jnp.split(z, 3, axis=-1)
    xin = c_gate * hh
    y = lax.conv_general_dilated(xin, cw[:, None, :], window_strides=(1,),
                                 padding=[(CONV_W - 1, 0)],
                                 dimension_numbers=('NWC', 'WIO', 'NWC'),
                                 feature_group_count=MIX_W)
    return b_gate * y


def parallel_mixer(h, cos, sin, lam_init, w_in, lam_vecs, subln, ln_g, ln_b, ws, bs,
                   pw, ps, cw, wb, wg, bg, wo):
    bsz, s, _ = h.shape
    zq, zk, zv, zg, zp, zc = jnp.split(h @ w_in, SPLIT_POINTS, axis=-1)
    q = apply_partial_rotary(zq.reshape(bsz, s, DA_HEADS, 2, DA_DK), cos, sin)
    k = apply_partial_rotary(zk.reshape(bsz, s, DA_HEADS, 2, DA_DK), cos, sin)
    v = zv.reshape(bsz, s, DA_HEADS, DA_DV)
    lv = lam_vecs.astype(jnp.float32)
    lam = jnp.exp(jnp.sum(lv[0] * lv[1])) - jnp.exp(jnp.sum(lv[2] * lv[3])) + lam_init
    o = differential_attention(q, k, v, lam)
    y_a = (rmsnorm(o, subln) * (1.0 - lam_init)).reshape(bsz, s, MIX_W)
    y_b = spatial_gating(zg, ln_g, ln_b, ws, bs)
    y_c = multiscale_pool(zp, pw, ps)
    y_d = short_conv_mixer(zc, cw)
    merged = jax.nn.sigmoid(h @ wg[0] + bg[0]) * (y_a @ wb[0])
    merged = merged + jax.nn.sigmoid(h @ wg[1] + bg[1]) * (y_b @ wb[1])
    merged = merged + jax.nn.sigmoid(h @ wg[2] + bg[2]) * (y_c @ wb[2])
    merged = merged + jax.nn.sigmoid(h @ wg[3] + bg[3]) * (y_d @ wb[3])
    return merged @ wo


def moe_ffn(h, router_w, router_b, w_gu, b_gu, w_dn, b_dn):
    bsz, s, d = h.shape
    n_tok = bsz * s
    xt = h.reshape(n_tok, d)
    logits = (xt @ router_w + router_b).astype(jnp.float32)
    top_v, top_i = lax.top_k(logits, TOP_K)
    wts = jax.nn.softmax(top_v, axis=-1)
    n_asg = n_tok * TOP_K
    flat_e = top_i.reshape(n_asg)
    order = jnp.argsort(flat_e)
    sorted_e = flat_e[order]
    tok = order // TOP_K
    counts = jnp.bincount(flat_e, length=N_EXPERTS)
    padded = ((counts + MOE_BLOCK - 1) // MOE_BLOCK) * MOE_BLOCK
    start = jnp.cumsum(counts) - counts
    pend = jnp.cumsum(padded)
    pstart = pend - padded
    dest = pstart[sorted_e] + jnp.arange(n_asg) - start[sorted_e]
    n_rows = ((n_asg + N_EXPERTS * MOE_BLOCK + MOE_BLOCK - 1) // MOE_BLOCK) * MOE_BLOCK
    n_blk = n_rows // MOE_BLOCK
    buf_tok = jnp.zeros((n_rows,), jnp.int32).at[dest].set(tok.astype(jnp.int32))
    buf_w = jnp.zeros((n_rows,), jnp.float32).at[dest].set(wts.reshape(n_asg)[order])
    block_e = jnp.minimum(jnp.searchsorted(pend, jnp.arange(n_blk) * MOE_BLOCK, side='right'),
                          N_EXPERTS - 1)
    xb = xt[buf_tok].reshape(n_blk, MOE_BLOCK, d)

    def expert_block(args):
        xe, e = args
        gu = xe @ w_gu[e] + b_gu[e]
        g, u = gu[:, :D_EXPERT], gu[:, D_EXPERT:]
        g = jnp.minimum(g, SWIGLU_LIMIT)
        u = jnp.clip(u, -SWIGLU_LIMIT, SWIGLU_LIMIT)
        act = (u + 1.0) * (g * jax.nn.sigmoid(SWIGLU_ALPHA * g))
        return act @ w_dn[e] + b_dn[e]

    out = lax.map(expert_block, (xb, block_e)).reshape(n_rows, d)
    y = jnp.zeros((n_tok, d), out.dtype).at[buf_tok].add(out * buf_w[:, None].astype(out.dtype))
    return y.reshape(bsz, s, d)


def setup_inputs(seed: int = 0) -> dict:
    key = jax.random.key(seed)
    ks = iter(jax.random.split(key, 40))

    def nrm(shape, scale):
        return jax.random.normal(next(ks), shape, jnp.float32) * scale

    L, D = DEPTH, D_MODEL
    x = nrm((BATCH, SEQ, D), 1.0)
    c = nrm((BATCH, D), 1.0)
    positions = (jnp.arange(SEQ, dtype=jnp.int32)[None, :]
                 + jax.random.randint(next(ks), (BATCH, 1), 0, 1024, dtype=jnp.int32))
    return {
        'x': x,
        'c': c,
        'positions': positions,
        'w_ada': nrm((L, D, 6 * D), 0.5 * D ** -0.5),
        'b_ada': nrm((L, 6 * D), 0.05),
        'norm_mix': 1.0 + nrm((L, D), 0.05),
        'norm_ffn': 1.0 + nrm((L, D), 0.05),
        'w_in': nrm((L, D, IN_COLS), D ** -0.5),
        'diff_lambda': nrm((L, 4, DA_DK), 0.1),
        'diff_subln': 1.0 + nrm((L, DA_DV), 0.05),
        'gmlp_ln_g': 1.0 + nrm((L, MIX_W), 0.05),
        'gmlp_ln_b': nrm((L, MIX_W), 0.02),
        'gmlp_w_spatial': nrm((L, GM_GROUPS, GM_CHUNK, GM_CHUNK), 0.5 * GM_CHUNK ** -0.5),
        'gmlp_b_spatial': 1.0 + nrm((L, GM_GROUPS, GM_CHUNK), 0.1),
        'pool_w': nrm((L, POOL_GROUPS, POOL_GC, POOL_GC), POOL_GC ** -0.5),
        'pool_scale': 1.0 + nrm((L, MIX_W), 0.1),
        'conv_w': nrm((L, CONV_W, MIX_W), CONV_W ** -0.5),
        'w_branch': nrm((L, N_BRANCH, MIX_W, D), MIX_W ** -0.5),
        'w_gate': nrm((L, N_BRANCH, D, D), D ** -0.5),
        'b_gate': nrm((L, N_BRANCH, D), 0.02),
        'w_out': nrm((L, D, D), D ** -0.5),
        'router_w': nrm((L, D, N_EXPERTS), D ** -0.5),
        'router_b': nrm((L, N_EXPERTS), 0.01),
        'expert_w_gu': nrm((L, N_EXPERTS, D, 2 * D_EXPERT), D ** -0.5),
        'expert_b_gu': nrm((L, N_EXPERTS, 2 * D_EXPERT), 0.01),
        'expert_w_down': nrm((L, N_EXPERTS, D_EXPERT, D), D_EXPERT ** -0.5),
        'expert_b_down': nrm((L, N_EXPERTS, D), 0.01),
        'final_norm': 1.0 + nrm((D,), 0.05),
    }


def reference(x, c, positions, w_ada, b_ada, norm_mix, norm_ffn, w_in, diff_lambda, diff_subln,
              gmlp_ln_g, gmlp_ln_b, gmlp_w_spatial, gmlp_b_spatial, pool_w, pool_scale, conv_w,
              w_branch, w_gate, b_gate, w_out, router_w, router_b, expert_w_gu, expert_b_gu,
              expert_w_down, expert_b_down, final_norm):
    cos, sin = rotary_tables(positions)
    c_act = jax.nn.silu(c)
    for l in range(DEPTH):
        lam_init = 0.8 - 0.6 * math.exp(-0.3 * l)
        mod = (c_act @ w_ada[l] + b_ada[l])[:, None, :]
        sh1, sc1, g1, sh2, sc2, g2 = jnp.split(mod, 6, axis=-1)
        h = rmsnorm(x, norm_mix[l]) * (1.0 + sc1) + sh1
        x = x + g1 * parallel_mixer(h, cos, sin, lam_init, w_in[l], diff_lambda[l], diff_subln[l],
                                    gmlp_ln_g[l], gmlp_ln_b[l], gmlp_w_spatial[l], gmlp_b_spatial[l],
                                    pool_w[l], pool_scale[l], conv_w[l], w_branch[l], w_gate[l],
                                    b_gate[l], w_out[l])
        h = rmsnorm(x, norm_ffn[l]) * (1.0 + sc2) + sh2
        x = x + g2 * moe_ffn(h, router_w[l], router_b[l], expert_w_gu[l], expert_b_gu[l],
                             expert_w_down[l], expert_b_down[l])
    return rmsnorm(x, final_norm)
```

```python
import functools
import math

import jax
import jax.numpy as jnp
from jax import lax
from jax.experimental import pallas as pl
from jax.experimental.pallas import tpu as pltpu

F32 = jnp.float32
BF16 = jnp.bfloat16
U32 = jnp.uint32
I32 = jnp.int32

MIX_W = 1024
DA_HEADS = 8
DA_DK = 64
DA_DV = 128
ROT_DIM = 16
ROPE_THETA = 500000.0
GM_GROUPS = 8
GM_CHUNK = 128
POOL_WINDOWS = (2, 4, 8, 16)
POOL_GC = 256
N_BRANCH = 4
N_EXPERTS = 32
TOP_K = 4
D_EXPERT = 1024
SWIGLU_LIMIT = 7.0
SWIGLU_ALPHA = 1.702
MOE_BLOCK = 512
EPS = 1e-6
NEG_INF = -1e30

LANES = 128
HALO = 16
VMEM_LIMIT = 56 * 1024 * 1024

COL_Q, COL_K, COL_V, COL_GU, COL_GV, COL_POOL, COL_CB, COL_CC, COL_CH = range(9)


def _params(*sem):
    return pltpu.CompilerParams(dimension_semantics=sem, vmem_limit_bytes=VMEM_LIMIT)


def _rms_mod(x, g, sc, sh):
    y = x * lax.rsqrt(jnp.mean(x * x, axis=-1, keepdims=True) + EPS) * g
    return y * (1.0 + sc) + sh


def _pack_pair(lo, hi):
    lo_bits = lax.bitcast_convert_type(lo.astype(BF16).astype(F32), U32) >> 16
    hi_bits = lax.bitcast_convert_type(hi.astype(BF16).astype(F32), U32) & jnp.uint32(0xFFFF0000)
    return hi_bits | lo_bits


def _unpack_pair(w):
    lo = lax.bitcast_convert_type(w << 16, F32)
    hi = lax.bitcast_convert_type(w & jnp.uint32(0xFFFF0000), F32)
    return lo, hi


def _adaln_kernel(c_ref, w_ref, b_ref, o_ref):
    c = c_ref[...]
    o_ref[...] = jnp.dot(c * jax.nn.sigmoid(c), w_ref[...], preferred_element_type=F32) + b_ref[...]


def _adaln(c_pad, w_ada, b_ada):
    depth, d, n = w_ada.shape
    bn = 1536
    rows = c_pad.shape[0]
    return pl.pallas_call(
        _adaln_kernel,
        grid=(depth, n // bn),
        in_specs=[pl.BlockSpec((rows, d), lambda l, j: (0, 0)),
                  pl.BlockSpec((None, d, bn), lambda l, j: (l, 0, j)),
                  pl.BlockSpec((None, 1, bn), lambda l, j: (l, 0, j))],
        out_specs=pl.BlockSpec((None, rows, bn), lambda l, j: (l, 0, j)),
        out_shape=jax.ShapeDtypeStruct((depth, rows, n), F32),
        compiler_params=_params("arbitrary", "arbitrary"),
        name="adaln",
    )(c_pad, w_ada, b_ada.reshape(depth, 1, n))


def _norm_kernel(x_ref, g_ref, sc_ref, sh_ref, h_ref):
    h_ref[...] = _rms_mod(x_ref[...], g_ref[...], sc_ref[...], sh_ref[...]).astype(BF16)


def _norm_mod(x, g, sc, sh, seq):
    t, d = x.shape
    bm = 512
    tps = seq // bm
    vec = pl.BlockSpec((None, 1, d), lambda i: (i // tps, 0, 0))
    return pl.pallas_call(
        _norm_kernel,
        grid=(t // bm,),
        in_specs=[pl.BlockSpec((bm, d), lambda i: (i, 0)),
                  pl.BlockSpec((1, d), lambda i: (0, 0)), vec, vec],
        out_specs=pl.BlockSpec((bm, d), lambda i: (i, 0)),
        out_shape=jax.ShapeDtypeStruct((t, d), BF16),
        compiler_params=_params("arbitrary"),
        name="norm_mod",
    )(x, g, sc, sh)


def _inproj_kernel(h_ref, w_ref, c_ref, s1_ref, s2_ref, z_ref, *, bn):
    j = pl.program_id(1)
    acc = jnp.dot(h_ref[...], w_ref[...], preferred_element_type=F32)
    qk_tiles = MIX_W // bn

    @pl.when(j >= 2 * qk_tiles)
    def _():
        z_ref[...] = acc.astype(z_ref.dtype)

    @pl.when(j < 2 * qk_tiles)
    def _():
        scale = jnp.where(j < qk_tiles, DA_DK ** -0.5, 1.0).astype(F32)
        cc = c_ref[...] * scale
        s1 = s1_ref[...] * scale
        s2 = s2_ref[...] * scale
        for t in range(bn // LANES):
            xt = acc[:, t * LANES:(t + 1) * LANES]
            half = ROT_DIM // 2
            r = xt * cc + pltpu.roll(xt, half, 1) * s1 + pltpu.roll(xt, LANES - half, 1) * s2
            z_ref[:, t * LANES:(t + 1) * LANES] = r.astype(z_ref.dtype)


def _inproj(h, w_in, rot_c, rot_s1, rot_s2):
    t, d = h.shape
    n = w_in.shape[1]
    bm, bn = 1024, 1024
    rot = pl.BlockSpec((bm, LANES), lambda i, j: (i, 0))
    return pl.pallas_call(
        functools.partial(_inproj_kernel, bn=bn),
        grid=(t // bm, n // bn),
        in_specs=[pl.BlockSpec((bm, d), lambda i, j: (i, 0)),
                  pl.BlockSpec((d, bn), lambda i, j: (0, j)), rot, rot, rot],
        out_specs=pl.BlockSpec((bm, bn), lambda i, j: (i, j)),
        out_shape=jax.ShapeDtypeStruct((t, n), BF16),
        compiler_params=_params("arbitrary", "arbitrary"),
        name="inproj",
    )(h, w_in, rot_c, rot_s1, rot_s2)


def _attn_kernel(q_ref, kt_ref, v_ref, lam_ref, sub_ref, o_ref, m_sc, l_sc, acc_sc, *, tq, lam_init):
    i = pl.program_id(2)
    q = q_ref[...]
    lane = lax.broadcasted_iota(I32, q.shape, 1)
    zero = jnp.zeros_like(q)
    qs = jnp.concatenate([jnp.where(lane < DA_DK, q, zero), jnp.where(lane >= DA_DK, q, zero)], axis=0)
    m_sc[...] = jnp.full(m_sc.shape, NEG_INF, F32)
    l_sc[...] = jnp.zeros(l_sc.shape, F32)
    acc_sc[...] = jnp.zeros(acc_sc.shape, F32)

    def step(j, masked):
        off = pl.multiple_of(j * tq, tq)
        s = jnp.dot(qs, kt_ref[:, pl.ds(off, tq)], preferred_element_type=F32)
        if masked:
            r2 = lax.broadcasted_iota(I32, s.shape, 0)
            col = lax.broadcasted_iota(I32, s.shape, 1)
            s = jnp.where(col <= jnp.where(r2 >= tq, r2 - tq, r2), s, NEG_INF)
        m_prev = m_sc[...]
        m_new = jnp.maximum(m_prev, jnp.max(s, axis=-1, keepdims=True))
        alpha = jnp.exp(m_prev - m_new)
        p = jnp.exp(s - m_new)
        l_sc[...] = alpha * l_sc[...] + jnp.sum(p, axis=-1, keepdims=True)
        acc_sc[...] = alpha * acc_sc[...] + jnp.dot(p.astype(BF16), v_ref[pl.ds(off, tq), :],
                                                    preferred_element_type=F32)
        m_sc[...] = m_new

    def body(j, carry):
        step(j, False)
        return carry

    lax.fori_loop(0, i, body, 0)
    step(i, True)

    lv = lam_ref[...]
    lam = (jnp.exp(jnp.sum(lv[0:1] * lv[1:2], axis=-1, keepdims=True))
           - jnp.exp(jnp.sum(lv[2:3] * lv[3:4], axis=-1, keepdims=True)) + lam_init)
    acc = acc_sc[...]
    l = l_sc[...]
    o = acc[:tq] / l[:tq] - lam * (acc[tq:] / l[tq:])
    y = o * lax.rsqrt(jnp.mean(o * o, axis=-1, keepdims=True) + EPS) * sub_ref[...] * (1.0 - lam_init)
    o_ref[...] = y.astype(o_ref.dtype)


def _attention(z, kt, lam_vecs, subln, bsz, seq, lam_init):
    t = z.shape[0]
    tq = min(512, seq)
    nq = seq // tq
    return pl.pallas_call(
        functools.partial(_attn_kernel, tq=tq, lam_init=lam_init),
        grid=(bsz, DA_HEADS, nq),
        in_specs=[pl.BlockSpec((tq, DA_DV), lambda b, h, i: (b * nq + i, COL_Q * DA_HEADS + h)),
                  pl.BlockSpec((DA_DV, seq), lambda b, h, i: (h, b)),
                  pl.BlockSpec((seq, DA_DV), lambda b, h, i: (b, COL_V * DA_HEADS + h)),
                  pl.BlockSpec((4, DA_DK), lambda b, h, i: (0, 0)),
                  pl.BlockSpec((1, DA_DV), lambda b, h, i: (0, 0))],
        out_specs=pl.BlockSpec((tq, DA_DV), lambda b, h, i: (b * nq + i, h)),
        out_shape=jax.ShapeDtypeStruct((t, MIX_W), BF16),
        scratch_shapes=[pltpu.VMEM((2 * tq, 1), F32), pltpu.VMEM((2 * tq, 1), F32),
                        pltpu.VMEM((2 * tq, DA_DV), F32)],
        compiler_params=_params("arbitrary", "arbitrary", "arbitrary"),
        name="diff_attn",
    )(z, kt, z, lam_vecs, subln)


def _gelu(x):
    return 0.5 * x * (1.0 + lax.erf(x * math.sqrt(0.5)))


def _gmlp_kernel(u_ref, v_ref, g_ref, b_ref, ws_ref, bs_ref, o_ref, *, bm):
    v = _gelu(v_ref[...].astype(F32))
    mu = jnp.mean(v, axis=-1, keepdims=True)
    vc = v - mu
    var = jnp.mean(vc * vc, axis=-1, keepdims=True)
    vn = (vc * lax.rsqrt(var + EPS) * g_ref[...] + b_ref[...]).astype(BF16)
    r = lax.broadcasted_iota(I32, (GM_CHUNK, GM_CHUNK), 0)
    c = lax.broadcasted_iota(I32, (GM_CHUNK, GM_CHUNK), 1)
    for g in range(GM_GROUPS):
        cols = slice(g * GM_CHUNK, (g + 1) * GM_CHUNK)
        wm = jnp.where(c <= r, ws_ref[g], 0.0).astype(BF16)
        bias = bs_ref[g]
        for n in range(bm // GM_CHUNK):
            rows = slice(n * GM_CHUNK, (n + 1) * GM_CHUNK)
            sv = jnp.dot(wm, vn[rows, cols], preferred_element_type=F32) + bias
            u = _gelu(u_ref[rows, cols].astype(F32))
            o_ref[rows, cols] = (u * sv).astype(o_ref.dtype)


def _gmlp(z, ln_g, ln_b, ws, bs_b):
    t = z.shape[0]
    bm = 512
    full = lambda shape: pl.BlockSpec(shape, lambda i: (0,) * len(shape))
    return pl.pallas_call(
        functools.partial(_gmlp_kernel, bm=bm),
        grid=(t // bm,),
        in_specs=[pl.BlockSpec((bm, MIX_W), lambda i: (i, COL_GU)),
                  pl.BlockSpec((bm, MIX_W), lambda i: (i, COL_GV)),
                  full((1, MIX_W)), full((1, MIX_W)),
                  full((GM_GROUPS, GM_CHUNK, GM_CHUNK)), full((GM_GROUPS, GM_CHUNK, GM_CHUNK))],
        out_specs=pl.BlockSpec((bm, MIX_W), lambda i: (i, 0)),
        out_shape=jax.ShapeDtypeStruct((t, MIX_W), BF16),
        compiler_params=_params("arbitrary"),
        name="gmlp",
    )(z, z, ln_g, ln_b, ws, bs_b)


def _pool_kernel(z_ref, halo_ref, pw_ref, ps_ref, o_ref, *, bm, tps):
    p0 = (pl.program_id(0) % tps) * bm
    d = (lax.broadcasted_iota(I32, (bm, bm), 0) - lax.broadcasted_iota(I32, (bm, bm), 1))
    dh = (lax.broadcasted_iota(I32, (bm, HALO), 0) + HALO - lax.broadcasted_iota(I32, (bm, HALO), 1))
    pos = p0 + lax.broadcasted_iota(I32, (bm, 1), 0)
    for g, w in enumerate(POOL_WINDOWS):
        cols = slice(g * POOL_GC, (g + 1) * POOL_GC)
        x = z_ref[:, cols]
        hx = halo_ref[:, cols]
        hx = jnp.where(p0 > 0, hx, jnp.zeros_like(hx))
        band = jnp.where(d >= 0, jnp.where(d < w, 1.0, 0.0), 0.0).astype(BF16)
        band_h = jnp.where(dh < w, 1.0, 0.0).astype(BF16)
        win = (jnp.dot(band, x, preferred_element_type=F32)
               + jnp.dot(band_h, hx, preferred_element_type=F32))
        cnt = jnp.minimum(pos + 1, w).astype(F32)
        pooled = (win / cnt - x.astype(F32)).astype(BF16)
        mixed = jnp.dot(pooled, pw_ref[g], preferred_element_type=F32)
        o_ref[:, cols] = (mixed * ps_ref[:, cols]).astype(o_ref.dtype)


def _halo_spec(bm, col):
    return pl.BlockSpec((HALO, MIX_W), lambda i: (jnp.maximum(i * (bm // HALO) - 1, 0), col))


def _pool(z, pw, ps, seq):
    t = z.shape[0]
    bm = min(512, seq)
    ng = len(POOL_WINDOWS)
    return pl.pallas_call(
        functools.partial(_pool_kernel, bm=bm, tps=seq // bm),
        grid=(t // bm,),
        in_specs=[pl.BlockSpec((bm, MIX_W), lambda i: (i, COL_POOL)),
                  _halo_spec(bm, COL_POOL),
                  pl.BlockSpec((ng, POOL_GC, POOL_GC), lambda i: (0, 0, 0)),
                  pl.BlockSpec((1, MIX_W), lambda i: (0, 0))],
        out_specs=pl.BlockSpec((bm, MIX_W), lambda i: (i, 0)),
        out_shape=jax.ShapeDtypeStruct((t, MIX_W), BF16),
        compiler_params=_params("arbitrary"),
        name="pool",
    )(z, z, pw, ps)


def _conv_kernel(b_ref, c_ref, h_ref, hc_ref, hh_ref, cw_ref, o_ref, *, tps):
    first = (pl.program_id(0) % tps) == 0
    xin = c_ref[...].astype(F32) * h_ref[...].astype(F32)
    hal = hc_ref[...].astype(F32) * hh_ref[...].astype(F32)
    hal = jnp.where(first, 0.0, hal)
    hm1 = hal[HALO - 1:HALO, :]
    hm2 = hal[HALO - 2:HALO - 1, :]
    row = lax.broadcasted_iota(I32, xin.shape, 0)
    s1 = jnp.where(row == 0, hm1, pltpu.roll(xin, 1, 0))
    s2 = jnp.where(row == 0, hm2, jnp.where(row == 1, hm1, pltpu.roll(xin, 2, 0)))
    cw = cw_ref[...]
    y = cw[0:1] * s2 + cw[1:2] * s1 + cw[2:3] * xin
    o_ref[...] = (b_ref[...].astype(F32) * y).astype(o_ref.dtype)


def _conv(z, cw, seq):
    t = z.shape[0]
    bm = min(512, seq)
    blk = lambda col: pl.BlockSpec((bm, MIX_W), lambda i: (i, col))
    return pl.pallas_call(
        functools.partial(_conv_kernel, tps=seq // bm),
        grid=(t // bm,),
        in_specs=[blk(COL_CB), blk(COL_CC), blk(COL_CH), _halo_spec(bm, COL_CC), _halo_spec(bm, COL_CH),
                  pl.BlockSpec(cw.shape, lambda i: (0, 0))],
        out_specs=pl.BlockSpec((bm, MIX_W), lambda i: (i, 0)),
        out_shape=jax.ShapeDtypeStruct((t, MIX_W), BF16),
        compiler_params=_params("arbitrary"),
        name="short_conv",
    )(z, z, z, z, z, cw)


def _merge_kernel(h_ref, ya_ref, yb_ref, yc_ref, yd_ref, wg_ref, bg_ref, wb_ref, o_ref):
    h = h_ref[...]
    bg = bg_ref[...]
    merged = None
    for i, y_ref in enumerate((ya_ref, yb_ref, yc_ref, yd_ref)):
        gate = jax.nn.sigmoid(jnp.dot(h, wg_ref[i], preferred_element_type=F32) + bg[i:i + 1])
        term = gate * jnp.dot(y_ref[...], wb_ref[i], preferred_element_type=F32)
        merged = term if merged is None else merged + term
    o_ref[...] = merged.astype(o_ref.dtype)


def _merge(h, ys, wg, bg, wb):
    t, d = h.shape
    bm, bn = 1024, 256
    yspec = pl.BlockSpec((bm, MIX_W), lambda i, j: (i, 0))
    return pl.pallas_call(
        _merge_kernel,
        grid=(t // bm, d // bn),
        in_specs=[pl.BlockSpec((bm, d), lambda i, j: (i, 0)), yspec, yspec, yspec, yspec,
                  pl.BlockSpec((N_BRANCH, d, bn), lambda i, j: (0, 0, j)),
                  pl.BlockSpec((N_BRANCH, bn), lambda i, j: (0, j)),
                  pl.BlockSpec((N_BRANCH, MIX_W, bn), lambda i, j: (0, 0, j))],
        out_specs=pl.BlockSpec((bm, bn), lambda i, j: (i, j)),
        out_shape=jax.ShapeDtypeStruct((t, d), BF16),
        compiler_params=_params("arbitrary", "arbitrary"),
        name="branch_merge",
    )(h, *ys, wg, bg, wb)


def _outproj_kernel(m_ref, wo_ref, x_ref, g1_ref, ng_ref, sc_ref, sh_ref, rw_ref, rb_ref,
                    xo_ref, hp_ref, ti_ref, tw_ref):
    out = jnp.dot(m_ref[...], wo_ref[...], preferred_element_type=F32)
    x_new = x_ref[...] + g1_ref[...] * out
    xo_ref[...] = x_new
    h2 = _rms_mod(x_new, ng_ref[...], sc_ref[...], sh_ref[...])
    half = h2.shape[1] // 2
    hp_ref[...] = _pack_pair(h2[:, :half], h2[:, half:])

    logits = jnp.dot(h2, rw_ref[...], preferred_element_type=F32) + rb_ref[...]
    lane = lax.broadcasted_iota(I32, logits.shape, 1).astype(F32)
    vals, idxs = [], []
    cur = logits
    for _ in range(TOP_K):
        m = jnp.max(cur, axis=-1, keepdims=True)
        idx = jnp.min(jnp.where(cur == m, lane, float(LANES)), axis=-1, keepdims=True)
        vals.append(m)
        idxs.append(idx)
        cur = jnp.where(lane == idx, -3e38, cur)
    es = [jnp.exp(v - vals[0]) for v in vals]
    den = es[0] + es[1] + es[2] + es[3]
    ti = jnp.zeros(logits.shape, F32)
    tw = jnp.zeros(logits.shape, F32)
    for k in range(TOP_K):
        ti = jnp.where(lane == float(k), idxs[k], ti)
        tw = jnp.where(lane == float(k), es[k] / den, tw)
    ti_ref[...] = ti.astype(I32)
    tw_ref[...] = tw


def _outproj(merged, wo, x, g1, ng, sc, sh, rw, rb, seq):
    t, d = x.shape
    bm = 512
    tps = seq // bm
    row = lambda w: pl.BlockSpec((bm, w), lambda i: (i, 0))
    vec = pl.BlockSpec((None, 1, d), lambda i: (i // tps, 0, 0))
    full = lambda a: pl.BlockSpec(a.shape, lambda i: (0, 0))
    return pl.pallas_call(
        _outproj_kernel,
        grid=(t // bm,),
        in_specs=[row(d), full(wo), row(d), vec, full(ng), vec, vec, full(rw), full(rb)],
        out_specs=[row(d), row(d // 2), row(LANES), row(LANES)],
        out_shape=[jax.ShapeDtypeStruct((t, d), F32), jax.ShapeDtypeStruct((t, d // 2), U32),
                   jax.ShapeDtypeStruct((t, LANES), I32), jax.ShapeDtypeStruct((t, LANES), F32)],
        compiler_params=_params("arbitrary"),
        name="outproj_router",
    )(merged, wo, x, g1, ng, sc, sh, rw, rb)


def _dispatch_kernel(nu_ref, tok_ref, hp_hbm, o_ref, sem):
    b = pl.program_id(0)
    rows = o_ref.shape[0]

    @pl.when(b < nu_ref[0])
    def _():
        def issue(r, carry):
            tok = tok_ref[0, 0, r]
            pltpu.make_async_copy(hp_hbm.at[pl.ds(tok, 1), :], o_ref.at[pl.ds(r, 1), :], sem).start()
            return carry

        lax.fori_loop(0, rows, issue, 0)
        pltpu.make_async_copy(hp_hbm.at[pl.ds(0, rows), :], o_ref, sem).wait()

    @pl.when(b >= nu_ref[0])
    def _():
        o_ref[...] = jnp.zeros(o_ref.shape, o_ref.dtype)


def _dispatch(n_used, buf_tok, hp):
    n_rows = buf_tok.shape[0]
    n_blk = n_rows // MOE_BLOCK
    w = hp.shape[1]
    return pl.pallas_call(
        _dispatch_kernel,
        grid_spec=pltpu.PrefetchScalarGridSpec(
            num_scalar_prefetch=1,
            grid=(n_blk,),
            in_specs=[pl.BlockSpec((1, 1, MOE_BLOCK), lambda b, nu: (b, 0, 0), memory_space=pltpu.SMEM),
                      pl.BlockSpec(memory_space=pl.ANY)],
            out_specs=pl.BlockSpec((MOE_BLOCK, w), lambda b, nu: (b, 0)),
            scratch_shapes=[pltpu.SemaphoreType.DMA(())]),
        out_shape=jax.ShapeDtypeStruct((n_rows, w), U32),
        compiler_params=_params("arbitrary"),
        name="moe_dispatch",
    )(n_used, buf_tok.reshape(n_blk, 1, MOE_BLOCK), hp)


def _expert_kernel(be_ref, nu_ref, xb_ref, wgu_ref, bgu_ref, wdn_ref, bdn_ref, o_ref):
    b = pl.program_id(0)

    @pl.when(b < nu_ref[0])
    def _():
        lo, hi = _unpack_pair(xb_ref[...])
        half = lo.shape[1]
        gu = (jnp.dot(lo.astype(BF16), wgu_ref[:half, :], preferred_element_type=F32)
              + jnp.dot(hi.astype(BF16), wgu_ref[half:, :], preferred_element_type=F32) + bgu_ref[...])
        g = jnp.minimum(gu[:, :D_EXPERT], SWIGLU_LIMIT)
        u = jnp.clip(gu[:, D_EXPERT:], -SWIGLU_LIMIT, SWIGLU_LIMIT)
        act = (u + 1.0) * (g * jax.nn.sigmoid(SWIGLU_ALPHA * g))
        out = jnp.dot(act.astype(BF16), wdn_ref[...], preferred_element_type=F32) + bdn_ref[...]
        o_ref[...] = _pack_pair(out[:, :half], out[:, half:])

    @pl.when(b >= nu_ref[0])
    def _():
        o_ref[...] = jnp.zeros(o_ref.shape, o_ref.dtype)


def _experts(block_e, n_used, xb, w_gu, b_gu, w_dn, b_dn):
    n_rows, w = xb.shape
    d = 2 * w
    n_blk = n_rows // MOE_BLOCK
    ne = w_gu.shape[0]
    return pl.pallas_call(
        _expert_kernel,
        grid_spec=pltpu.PrefetchScalarGridSpec(
            num_scalar_prefetch=2,
            grid=(n_blk,),
            in_specs=[pl.BlockSpec((MOE_BLOCK, w), lambda b, be, nu: (b, 0)),
                      pl.BlockSpec((None, d, 2 * D_EXPERT), lambda b, be, nu: (be[b], 0, 0)),
                      pl.BlockSpec((None, 1, 2 * D_EXPERT), lambda b, be, nu: (be[b], 0, 0)),
                      pl.BlockSpec((None, D_EXPERT, d), lambda b, be, nu: (be[b], 0, 0)),
                      pl.BlockSpec((None, 1, d), lambda b, be, nu: (be[b], 0, 0))],
            out_specs=pl.BlockSpec((MOE_BLOCK, w), lambda b, be, nu: (b, 0))),
        out_shape=jax.ShapeDtypeStruct((n_rows, w), U32),
        compiler_params=_params("arbitrary"),
        name="moe_experts",
    )(block_e, n_used, xb, w_gu, b_gu.reshape(ne, 1, -1), w_dn, b_dn.reshape(ne, 1, -1))


def _combine_kernel(dest_ref, tw_ref, x_ref, g2_ref, ng_ref, sc_ref, sh_ref, outp_hbm, *rest, bm, final):
    if final:
        ho_ref, buf, sem = rest
    else:
        xo_ref, ho_ref, buf, sem = rest

    def issue(r, carry):
        for k in range(TOP_K):
            dst = dest_ref[0, 0, r * TOP_K + k]
            pltpu.make_async_copy(outp_hbm.at[pl.ds(dst, 1), :], buf.at[k, pl.ds(r, 1), :], sem).start()
        return carry

    lax.fori_loop(0, bm, issue, 0)
    for k in range(TOP_K):
        pltpu.make_async_copy(outp_hbm.at[pl.ds(0, bm), :], buf.at[k], sem).wait()

    tw = tw_ref[...]
    y_lo = y_hi = None
    for k in range(TOP_K):
        lo, hi = _unpack_pair(buf[k])
        wk = tw[:, k:k + 1]
        y_lo = wk * lo if y_lo is None else y_lo + wk * lo
        y_hi = wk * hi if y_hi is None else y_hi + wk * hi
    x_new = x_ref[...] + g2_ref[...] * jnp.concatenate([y_lo, y_hi], axis=1)
    if final:
        ho_ref[...] = x_new * lax.rsqrt(jnp.mean(x_new * x_new, axis=-1, keepdims=True) + EPS) * ng_ref[...]
    else:
        xo_ref[...] = x_new
        ho_ref[...] = _rms_mod(x_new, ng_ref[...], sc_ref[...], sh_ref[...]).astype(BF16)


def _combine(dest, tw, x, g2, ng, sc, sh, outp, seq, final):
    t, d = x.shape
    bm = 256
    tps = seq // bm
    nt = t // bm
    row = lambda w: pl.BlockSpec((bm, w), lambda i: (i, 0))
    vec = pl.BlockSpec((None, 1, d), lambda i: (i // tps, 0, 0))
    if final:
        out_specs = [row(d)]
        out_shape = [jax.ShapeDtypeStruct((t, d), F32)]
    else:
        out_specs = [row(d), row(d)]
        out_shape = [jax.ShapeDtypeStruct((t, d), F32), jax.ShapeDtypeStruct((t, d), BF16)]
    return pl.pallas_call(
        functools.partial(_combine_kernel, bm=bm, final=final),
        grid=(nt,),
        in_specs=[pl.BlockSpec((1, 1, bm * TOP_K), lambda i: (i, 0, 0), memory_space=pltpu.SMEM),
                  row(LANES), row(d), vec, pl.BlockSpec((1, d), lambda i: (0, 0)), vec, vec,
                  pl.BlockSpec(memory_space=pl.ANY)],
        out_specs=out_specs,
        out_shape=out_shape,
        scratch_shapes=[pltpu.VMEM((TOP_K, bm, d // 2), U32), pltpu.SemaphoreType.DMA(())],
        compiler_params=_params("arbitrary"),
        name="moe_combine",
    )(dest.reshape(nt, 1, bm * TOP_K), tw, x, g2, ng, sc, sh, outp)


def _routing_tables(top_i):
    t = top_i.shape[0]
    n_asg = t * TOP_K
    flat_e = top_i.reshape(n_asg)
    onehot = (flat_e[:, None] == jnp.arange(N_EXPERTS, dtype=I32)[None, :]).astype(I32)
    csum = jnp.cumsum(onehot, axis=0)
    rank = jnp.take_along_axis(csum, flat_e[:, None], axis=1)[:, 0] - 1
    counts = csum[-1]
    padded = ((counts + MOE_BLOCK - 1) // MOE_BLOCK) * MOE_BLOCK
    pend = jnp.cumsum(padded)
    pstart = pend - padded
    dest = (pstart[flat_e] + rank).astype(I32)
    n_rows = ((n_asg + N_EXPERTS * MOE_BLOCK + MOE_BLOCK - 1) // MOE_BLOCK) * MOE_BLOCK
    n_blk = n_rows // MOE_BLOCK
    buf_tok = jnp.zeros((n_rows,), I32).at[dest].set(jnp.arange(n_asg, dtype=I32) // TOP_K)
    block_e = jnp.minimum(jnp.searchsorted(pend, jnp.arange(n_blk, dtype=I32) * MOE_BLOCK, side='right'),
                          N_EXPERTS - 1).astype(I32)
    n_used = (pend[-1:] // MOE_BLOCK).astype(I32)
    return dest, buf_tok, block_e, n_used


def _rotary_tables(positions):
    half = ROT_DIM // 2
    inv_freq = ROPE_THETA ** (-jnp.arange(0, ROT_DIM, 2, dtype=F32) / ROT_DIM)
    ang = positions.astype(F32).reshape(-1, 1) * inv_freq
    cos, sin = jnp.cos(ang), jnp.sin(ang)
    t = ang.shape[0]
    pad = jnp.zeros((t, DA_DK - ROT_DIM), F32)
    c64 = jnp.concatenate([cos, cos, pad + 1.0], axis=1)
    s1_64 = jnp.concatenate([jnp.zeros_like(sin), sin, pad], axis=1)
    s2_64 = jnp.concatenate([-sin, jnp.zeros_like(sin), pad], axis=1)
    rep = LANES // DA_DK
    return jnp.tile(c64, (1, rep)), jnp.tile(s1_64, (1, rep)), jnp.tile(s2_64, (1, rep))


def kernel(x, c, positions, w_ada, b_ada, norm_mix, norm_ffn, w_in, diff_lambda, diff_subln, gmlp_ln_g, gmlp_ln_b, gmlp_w_spatial, gmlp_b_spatial, pool_w, pool_scale, conv_w, w_branch, w_gate, b_gate, w_out, router_w, router_b, expert_w_gu, expert_b_gu, expert_w_down, expert_b_down, final_norm):
    bsz, seq, d = x.shape
    depth = w_ada.shape[0]
    t = bsz * seq
    assert seq % 512 == 0 and d == 2 * MIX_W

    rot_c, rot_s1, rot_s2 = _rotary_tables(positions)
    c_pad = jnp.zeros((8, d), F32).at[:bsz].set(c)
    mod = _adaln(c_pad, w_ada, b_ada)[:, :bsz].reshape(depth, bsz, 6, 1, d)
    rw_pad = jnp.zeros((depth, d, LANES), F32).at[:, :, :N_EXPERTS].set(router_w)
    rb_pad = jnp.full((depth, 1, LANES), NEG_INF, F32).at[:, 0, :N_EXPERTS].set(router_b)

    xf = x.reshape(t, d)
    h = None
    for l in range(depth):
        lam_init = 0.8 - 0.6 * math.exp(-0.3 * l)
        sh1, sc1, g1, sh2, sc2, g2 = (mod[l, :, i] for i in range(6))
        if l == 0:
            h = _norm_mod(xf, norm_mix[l].reshape(1, d), sc1, sh1, seq)
        z = _inproj(h, w_in[l].astype(BF16), rot_c, rot_s1, rot_s2)
        kt = z[:, COL_K * MIX_W:(COL_K + 1) * MIX_W].T
        y_a = _attention(z, kt, diff_lambda[l], diff_subln[l].reshape(1, DA_DV), bsz, seq, lam_init)
        bs_b = jnp.broadcast_to(gmlp_b_spatial[l][:, :, None], (GM_GROUPS, GM_CHUNK, GM_CHUNK))
        y_b = _gmlp(z, gmlp_ln_g[l].reshape(1, MIX_W), gmlp_ln_b[l].reshape(1, MIX_W),
                    gmlp_w_spatial[l], bs_b)
        y_c = _pool(z, pool_w[l].astype(BF16), pool_scale[l].reshape(1, MIX_W), seq)
        y_d = _conv(z, conv_w[l], seq)
        merged = _merge(h, (y_a, y_b, y_c, y_d), w_gate[l].astype(BF16), b_gate[l], w_branch[l].astype(BF16))
        xf, hp, ti, tw = _outproj(merged, w_out[l].astype(BF16), xf, g1, norm_ffn[l].reshape(1, d), sc2, sh2,
                                  rw_pad[l], rb_pad[l], seq)
        dest, buf_tok, block_e, n_used = _routing_tables(ti[:, :TOP_K])
        xb = _dispatch(n_used, buf_tok, hp)
        outp = _experts(block_e, n_used, xb, expert_w_gu[l].astype(BF16), expert_b_gu[l],
                        expert_w_down[l].astype(BF16), expert_b_down[l])
        if l + 1 < depth:
            nsh1, nsc1 = mod[l + 1, :, 0], mod[l + 1, :, 1]
            xf, h = _combine(dest, tw, xf, g2, norm_mix[l + 1].reshape(1, d), nsc1, nsh1, outp, seq, False)
        else:
            zero = jnp.zeros((bsz, 1, d), F32)
            (out,) = _combine(dest, tw, xf, g2, final_norm.reshape(1, d), zero, zero, outp, seq, True)
    return out.reshape(bsz, seq, d)
```

```python
import functools
import math

import jax
import jax.numpy as jnp
from jax import lax
from jax.experimental import pallas as pl
from jax.experimental.pallas import tpu as pltpu

F32 = jnp.float32
BF16 = jnp.bfloat16
U32 = jnp.uint32
I32 = jnp.int32

MIX_W = 1024
DA_HEADS = 8
DA_DK = 64
DA_DV = 128
ROT_DIM = 16
ROPE_THETA = 500000.0
GM_GROUPS = 8
GM_CHUNK = 128
POOL_WINDOWS = (2, 4, 8, 16)
POOL_GC = 256
N_BRANCH = 4
N_EXPERTS = 32
TOP_K = 4
D_EXPERT = 1024
SWIGLU_LIMIT = 7.0
SWIGLU_ALPHA = 1.702
MOE_BLOCK = 512
EPS = 1e-6
NEG_INF = -1e30

LANES = 128
HALO = 16
VMEM_LIMIT = 56 * 1024 * 1024

COL_Q, COL_K, COL_V, COL_GU, COL_GV, COL_POOL, COL_CB, COL_CC, COL_CH = range(9)


def _params(*sem):
    return pltpu.CompilerParams(dimension_semantics=sem, vmem_limit_bytes=VMEM_LIMIT)


def _rms_mod(x, g, sc, sh):
    y = x * lax.rsqrt(jnp.mean(x * x, axis=-1, keepdims=True) + EPS) * g
    return y * (1.0 + sc) + sh


def _pack_pair(lo, hi):
    lo_bits = lax.bitcast_convert_type(lo.astype(BF16).astype(F32), U32) >> 16
    hi_bits = lax.bitcast_convert_type(hi.astype(BF16).astype(F32), U32) & jnp.uint32(0xFFFF0000)
    return hi_bits | lo_bits


def _unpack_pair(w):
    lo = lax.bitcast_convert_type(w << 16, F32)
    hi = lax.bitcast_convert_type(w & jnp.uint32(0xFFFF0000), F32)
    return lo, hi


def _adaln_kernel(c_ref, w_ref, b_ref, o_ref):
    c = c_ref[...]
    o_ref[...] = jnp.dot(c * jax.nn.sigmoid(c), w_ref[...], preferred_element_type=F32) + b_ref[...]


def _adaln(c_pad, w_ada, b_ada):
    depth, d, n = w_ada.shape
    bn = 1536
    rows = c_pad.shape[0]
    return pl.pallas_call(
        _adaln_kernel,
        grid=(depth, n // bn),
        in_specs=[pl.BlockSpec((rows, d), lambda l, j: (0, 0)),
                  pl.BlockSpec((None, d, bn), lambda l, j: (l, 0, j)),
                  pl.BlockSpec((None, 1, bn), lambda l, j: (l, 0, j))],
        out_specs=pl.BlockSpec((None, rows, bn), lambda l, j: (l, 0, j)),
        out_shape=jax.ShapeDtypeStruct((depth, rows, n), F32),
        compiler_params=_params("arbitrary", "arbitrary"),
        name="adaln",
    )(c_pad, w_ada, b_ada.reshape(depth, 1, n))


def _norm_kernel(x_ref, g_ref, sc_ref, sh_ref, h_ref):
    h_ref[...] = _rms_mod(x_ref[...], g_ref[...], sc_ref[...], sh_ref[...]).astype(BF16)


def _norm_mod(x, g, sc, sh, seq):
    t, d = x.shape
    bm = 512
    tps = seq // bm
    vec = pl.BlockSpec((None, 1, d), lambda i: (i // tps, 0, 0))
    return pl.pallas_call(
        _norm_kernel,
        grid=(t // bm,),
        in_specs=[pl.BlockSpec((bm, d), lambda i: (i, 0)),
                  pl.BlockSpec((1, d), lambda i: (0, 0)), vec, vec],
        out_specs=pl.BlockSpec((bm, d), lambda i: (i, 0)),
        out_shape=jax.ShapeDtypeStruct((t, d), BF16),
        compiler_params=_params("arbitrary"),
        name="norm_mod",
    )(x, g, sc, sh)


def _inproj_kernel(h_ref, w_ref, c_ref, s1_ref, s2_ref, z_ref, *, bn):
    j = pl.program_id(1)
    acc = jnp.dot(h_ref[...], w_ref[...], preferred_element_type=F32)
    qk_tiles = MIX_W // bn

    @pl.when(j >= 2 * qk_tiles)
    def _():
        z_ref[...] = acc.astype(z_ref.dtype)

    @pl.when(j < 2 * qk_tiles)
    def _():
        scale = jnp.where(j < qk_tiles, DA_DK ** -0.5 * math.log2(math.e), 1.0).astype(F32)
        cc = c_ref[...] * scale
        s1 = s1_ref[...] * scale
        s2 = s2_ref[...] * scale
        for t in range(bn // LANES):
            xt = acc[:, t * LANES:(t + 1) * LANES]
            half = ROT_DIM // 2
            r = xt * cc + pltpu.roll(xt, half, 1) * s1 + pltpu.roll(xt, LANES - half, 1) * s2
            z_ref[:, t * LANES:(t + 1) * LANES] = r.astype(z_ref.dtype)


def _inproj(h, w_in, rot_c, rot_s1, rot_s2):
    t, d = h.shape
    n = w_in.shape[1]
    bm, bn = 1024, 1024
    rot = pl.BlockSpec((bm, LANES), lambda i, j: (i, 0))
    return pl.pallas_call(
        functools.partial(_inproj_kernel, bn=bn),
        grid=(t // bm, n // bn),
        in_specs=[pl.BlockSpec((bm, d), lambda i, j: (i, 0)),
                  pl.BlockSpec((d, bn), lambda i, j: (0, j)), rot, rot, rot],
        out_specs=pl.BlockSpec((bm, bn), lambda i, j: (i, j)),
        out_shape=jax.ShapeDtypeStruct((t, n), BF16),
        compiler_params=_params("arbitrary", "arbitrary"),
        name="inproj",
    )(h, w_in, rot_c, rot_s1, rot_s2)


def _attn_kernel(qt_ref, k_ref, vt_ref, lam_ref, sub_ref, o_ref, m_sc, l_sc, acc_sc, *, tq, lam_init):
    i = pl.program_id(2)
    qt = qt_ref[...]
    row = lax.broadcasted_iota(I32, qt.shape, 0)
    zero = jnp.zeros_like(qt)
    qs = jnp.concatenate([jnp.where(row < DA_DK, qt, zero), jnp.where(row >= DA_DK, qt, zero)], axis=1)
    m_sc[...] = jnp.full(m_sc.shape, NEG_INF, F32)
    l_sc[...] = jnp.zeros(l_sc.shape, F32)
    acc_sc[...] = jnp.zeros(acc_sc.shape, F32)

    def step(j, masked):
        off = pl.multiple_of(j * tq, tq)
        s = jnp.dot(k_ref[pl.ds(off, tq), :], qs, preferred_element_type=F32)
        if masked:
            key = lax.broadcasted_iota(I32, s.shape, 0)
            c2 = lax.broadcasted_iota(I32, s.shape, 1)
            s = jnp.where(key <= jnp.where(c2 >= tq, c2 - tq, c2), s, NEG_INF)
        m_prev = m_sc[...]
        m_new = jnp.maximum(m_prev, jnp.max(s, axis=0, keepdims=True))
        alpha = jnp.exp2(m_prev - m_new)
        p = jnp.exp2(s - m_new)
        l_sc[...] = alpha * l_sc[...] + jnp.sum(p, axis=0, keepdims=True)
        acc_sc[...] = alpha * acc_sc[...] + jnp.dot(vt_ref[:, pl.ds(off, tq)], p.astype(BF16),
                                                    preferred_element_type=F32)
        m_sc[...] = m_new

    def body(j, carry):
        step(j, False)
        return carry

    lax.fori_loop(0, i, body, 0)
    step(i, True)

    lv = lam_ref[...]
    lam = (jnp.exp(jnp.sum(lv[0:1] * lv[1:2], axis=-1, keepdims=True))
           - jnp.exp(jnp.sum(lv[2:3] * lv[3:4], axis=-1, keepdims=True)) + lam_init)
    acc = acc_sc[...]
    l = l_sc[...]
    o = acc[:, :tq] / l[:, :tq] - lam * (acc[:, tq:] / l[:, tq:])
    y = o * lax.rsqrt(jnp.mean(o * o, axis=0, keepdims=True) + EPS) * sub_ref[...] * (1.0 - lam_init)
    o_ref[...] = y.T.astype(o_ref.dtype)


def _attention(z, qt, vt, lam_vecs, subln_col, bsz, seq, lam_init):
    t = z.shape[0]
    tq = min(512, seq)
    nq = seq // tq
    return pl.pallas_call(
        functools.partial(_attn_kernel, tq=tq, lam_init=lam_init),
        grid=(bsz, DA_HEADS, nq),
        in_specs=[pl.BlockSpec((DA_DV, tq), lambda b, h, i: (h, b * nq + i)),
                  pl.BlockSpec((seq, DA_DV), lambda b, h, i: (b, COL_K * DA_HEADS + h)),
                  pl.BlockSpec((DA_DV, seq), lambda b, h, i: (h, b)),
                  pl.BlockSpec((4, DA_DK), lambda b, h, i: (0, 0)),
                  pl.BlockSpec((DA_DV, 1), lambda b, h, i: (0, 0))],
        out_specs=pl.BlockSpec((tq, DA_DV), lambda b, h, i: (b * nq + i, h)),
        out_shape=jax.ShapeDtypeStruct((t, MIX_W), BF16),
        scratch_shapes=[pltpu.VMEM((1, 2 * tq), F32), pltpu.VMEM((1, 2 * tq), F32),
                        pltpu.VMEM((DA_DV, 2 * tq), F32)],
        compiler_params=_params("arbitrary", "arbitrary", "arbitrary"),
        name="diff_attn",
    )(qt, z, vt, lam_vecs, subln_col)


def _gelu(x):
    return 0.5 * x * (1.0 + lax.erf(x * math.sqrt(0.5)))


def _gmlp_kernel(u_ref, v_ref, g_ref, b_ref, ws_ref, bs_ref, o_ref, *, bm):
    v = _gelu(v_ref[...].astype(F32))
    mu = jnp.mean(v, axis=-1, keepdims=True)
    vc = v - mu
    var = jnp.mean(vc * vc, axis=-1, keepdims=True)
    vn = (vc * lax.rsqrt(var + EPS) * g_ref[...] + b_ref[...]).astype(BF16)
    r = lax.broadcasted_iota(I32, (GM_CHUNK, GM_CHUNK), 0)
    c = lax.broadcasted_iota(I32, (GM_CHUNK, GM_CHUNK), 1)
    for g in range(GM_GROUPS):
        cols = slice(g * GM_CHUNK, (g + 1) * GM_CHUNK)
        wm = jnp.where(c <= r, ws_ref[g], 0.0).astype(BF16)
        bias = bs_ref[g]
        for n in range(bm // GM_CHUNK):
            rows = slice(n * GM_CHUNK, (n + 1) * GM_CHUNK)
            sv = jnp.dot(wm, vn[rows, cols], preferred_element_type=F32) + bias
            u = _gelu(u_ref[rows, cols].astype(F32))
            o_ref[rows, cols] = (u * sv).astype(o_ref.dtype)


def _gmlp(z, ln_g, ln_b, ws, bs_b):
    t = z.shape[0]
    bm = 512
    full = lambda shape: pl.BlockSpec(shape, lambda i: (0,) * len(shape))
    return pl.pallas_call(
        functools.partial(_gmlp_kernel, bm=bm),
        grid=(t // bm,),
        in_specs=[pl.BlockSpec((bm, MIX_W), lambda i: (i, COL_GU)),
                  pl.BlockSpec((bm, MIX_W), lambda i: (i, COL_GV)),
                  full((1, MIX_W)), full((1, MIX_W)),
                  full((GM_GROUPS, GM_CHUNK, GM_CHUNK)), full((GM_GROUPS, GM_CHUNK, GM_CHUNK))],
        out_specs=pl.BlockSpec((bm, MIX_W), lambda i: (i, 0)),
        out_shape=jax.ShapeDtypeStruct((t, MIX_W), BF16),
        compiler_params=_params("arbitrary"),
        name="gmlp",
    )(z, z, ln_g, ln_b, ws, bs_b)


def _pool_kernel(z_ref, halo_ref, pw_ref, ps_ref, o_ref, *, bm, tps):
    p0 = (pl.program_id(0) % tps) * bm
    d = (lax.broadcasted_iota(I32, (bm, bm), 0) - lax.broadcasted_iota(I32, (bm, bm), 1))
    dh = (lax.broadcasted_iota(I32, (bm, HALO), 0) + HALO - lax.broadcasted_iota(I32, (bm, HALO), 1))
    pos = p0 + lax.broadcasted_iota(I32, (bm, 1), 0)
    for g, w in enumerate(POOL_WINDOWS):
        cols = slice(g * POOL_GC, (g + 1) * POOL_GC)
        x = z_ref[:, cols]
        hx = halo_ref[:, cols]
        hx = jnp.where(p0 > 0, hx, jnp.zeros_like(hx))
        band = jnp.where(d >= 0, jnp.where(d < w, 1.0, 0.0), 0.0).astype(BF16)
        band_h = jnp.where(dh < w, 1.0, 0.0).astype(BF16)
        win = (jnp.dot(band, x, preferred_element_type=F32)
               + jnp.dot(band_h, hx, preferred_element_type=F32))
        cnt = jnp.minimum(pos + 1, w).astype(F32)
        pooled = (win / cnt - x.astype(F32)).astype(BF16)
        mixed = jnp.dot(pooled, pw_ref[g], preferred_element_type=F32)
        o_ref[:, cols] = (mixed * ps_ref[:, cols]).astype(o_ref.dtype)


def _halo_spec(bm, col):
    return pl.BlockSpec((HALO, MIX_W), lambda i: (jnp.maximum(i * (bm // HALO) - 1, 0), col))


def _pool(z, pw, ps, seq):
    t = z.shape[0]
    bm = min(512, seq)
    ng = len(POOL_WINDOWS)
    return pl.pallas_call(
        functools.partial(_pool_kernel, bm=bm, tps=seq // bm),
        grid=(t // bm,),
        in_specs=[pl.BlockSpec((bm, MIX_W), lambda i: (i, COL_POOL)),
                  _halo_spec(bm, COL_POOL),
                  pl.BlockSpec((ng, POOL_GC, POOL_GC), lambda i: (0, 0, 0)),
                  pl.BlockSpec((1, MIX_W), lambda i: (0, 0))],
        out_specs=pl.BlockSpec((bm, MIX_W), lambda i: (i, 0)),
        out_shape=jax.ShapeDtypeStruct((t, MIX_W), BF16),
        compiler_params=_params("arbitrary"),
        name="pool",
    )(z, z, pw, ps)


def _conv_kernel(b_ref, c_ref, h_ref, hc_ref, hh_ref, cw_ref, o_ref, *, tps):
    first = (pl.program_id(0) % tps) == 0
    xin = c_ref[...].astype(F32) * h_ref[...].astype(F32)
    hal = hc_ref[...].astype(F32) * hh_ref[...].astype(F32)
    hal = jnp.where(first, 0.0, hal)
    hm1 = hal[HALO - 1:HALO, :]
    hm2 = hal[HALO - 2:HALO - 1, :]
    row = lax.broadcasted_iota(I32, xin.shape, 0)
    s1 = jnp.where(row == 0, hm1, pltpu.roll(xin, 1, 0))
    s2 = jnp.where(row == 0, hm2, jnp.where(row == 1, hm1, pltpu.roll(xin, 2, 0)))
    cw = cw_ref[...]
    y = cw[0:1] * s2 + cw[1:2] * s1 + cw[2:3] * xin
    o_ref[...] = (b_ref[...].astype(F32) * y).astype(o_ref.dtype)


def _conv(z, cw, seq):
    t = z.shape[0]
    bm = min(512, seq)
    blk = lambda col: pl.BlockSpec((bm, MIX_W), lambda i: (i, col))
    return pl.pallas_call(
        functools.partial(_conv_kernel, tps=seq // bm),
        grid=(t // bm,),
        in_specs=[blk(COL_CB), blk(COL_CC), blk(COL_CH), _halo_spec(bm, COL_CC), _halo_spec(bm, COL_CH),
                  pl.BlockSpec(cw.shape, lambda i: (0, 0))],
        out_specs=pl.BlockSpec((bm, MIX_W), lambda i: (i, 0)),
        out_shape=jax.ShapeDtypeStruct((t, MIX_W), BF16),
        compiler_params=_params("arbitrary"),
        name="short_conv",
    )(z, z, z, z, z, cw)


def _merge_kernel(h_ref, ya_ref, yb_ref, yc_ref, yd_ref, wg_ref, bg_ref, wb_ref, o_ref):
    h = h_ref[...]
    bg = bg_ref[...]
    merged = None
    for i, y_ref in enumerate((ya_ref, yb_ref, yc_ref, yd_ref)):
        gate = jax.nn.sigmoid(jnp.dot(h, wg_ref[i], preferred_element_type=F32) + bg[i:i + 1])
        term = gate * jnp.dot(y_ref[...], wb_ref[i], preferred_element_type=F32)
        merged = term if merged is None else merged + term
    o_ref[...] = merged.astype(o_ref.dtype)


def _merge(h, ys, wg, bg, wb):
    t, d = h.shape
    bm, bn = 1024, 256
    yspec = pl.BlockSpec((bm, MIX_W), lambda i, j: (i, 0))
    return pl.pallas_call(
        _merge_kernel,
        grid=(t // bm, d // bn),
        in_specs=[pl.BlockSpec((bm, d), lambda i, j: (i, 0)), yspec, yspec, yspec, yspec,
                  pl.BlockSpec((N_BRANCH, d, bn), lambda i, j: (0, 0, j)),
                  pl.BlockSpec((N_BRANCH, bn), lambda i, j: (0, j)),
                  pl.BlockSpec((N_BRANCH, MIX_W, bn), lambda i, j: (0, 0, j))],
        out_specs=pl.BlockSpec((bm, bn), lambda i, j: (i, j)),
        out_shape=jax.ShapeDtypeStruct((t, d), BF16),
        compiler_params=_params("arbitrary", "arbitrary"),
        name="branch_merge",
    )(h, *ys, wg, bg, wb)


def _outproj_kernel(m_ref, wo_ref, x_ref, g1_ref, ng_ref, sc_ref, sh_ref, rw_ref, rb_ref,
                    xo_ref, hp_ref, ti_ref, tw_ref):
    out = jnp.dot(m_ref[...], wo_ref[...], preferred_element_type=F32)
    x_new = x_ref[...] + g1_ref[...] * out
    xo_ref[...] = x_new
    h2 = _rms_mod(x_new, ng_ref[...], sc_ref[...], sh_ref[...])
    half = h2.shape[1] // 2
    hp_ref[...] = _pack_pair(h2[:, :half], h2[:, half:])

    logits = jnp.dot(h2, rw_ref[...], preferred_element_type=F32) + rb_ref[...]
    lane = lax.broadcasted_iota(I32, logits.shape, 1).astype(F32)
    vals, idxs = [], []
    cur = logits
    for _ in range(TOP_K):
        m = jnp.max(cur, axis=-1, keepdims=True)
        idx = jnp.min(jnp.where(cur == m, lane, float(LANES)), axis=-1, keepdims=True)
        vals.append(m)
        idxs.append(idx)
        cur = jnp.where(lane == idx, -3e38, cur)
    es = [jnp.exp(v - vals[0]) for v in vals]
    den = es[0] + es[1] + es[2] + es[3]
    ti = jnp.zeros(logits.shape, F32)
    tw = jnp.zeros(logits.shape, F32)
    for k in range(TOP_K):
        ti = jnp.where(lane == float(k), idxs[k], ti)
        tw = jnp.where(lane == float(k), es[k] / den, tw)
    ti_ref[...] = ti.astype(I32)
    tw_ref[...] = tw


def _outproj(merged, wo, x, g1, ng, sc, sh, rw, rb, seq):
    t, d = x.shape
    bm = 512
    tps = seq // bm
    row = lambda w: pl.BlockSpec((bm, w), lambda i: (i, 0))
    vec = pl.BlockSpec((None, 1, d), lambda i: (i // tps, 0, 0))
    full = lambda a: pl.BlockSpec(a.shape, lambda i: (0, 0))
    return pl.pallas_call(
        _outproj_kernel,
        grid=(t // bm,),
        in_specs=[row(d), full(wo), row(d), vec, full(ng), vec, vec, full(rw), full(rb)],
        out_specs=[row(d), row(d // 2), row(LANES), row(LANES)],
        out_shape=[jax.ShapeDtypeStruct((t, d), F32), jax.ShapeDtypeStruct((t, d // 2), U32),
                   jax.ShapeDtypeStruct((t, LANES), I32), jax.ShapeDtypeStruct((t, LANES), F32)],
        compiler_params=_params("arbitrary"),
        name="outproj_router",
    )(merged, wo, x, g1, ng, sc, sh, rw, rb)


def _dispatch_kernel(nu_ref, tok_ref, hp_hbm, o_ref, sem):
    b = pl.program_id(0)
    rows = o_ref.shape[0]

    @pl.when(b < nu_ref[0])
    def _():
        def issue(r, carry):
            tok = tok_ref[0, 0, r]
            pltpu.make_async_copy(hp_hbm.at[pl.ds(tok, 1), :], o_ref.at[pl.ds(r, 1), :], sem).start()
            return carry

        lax.fori_loop(0, rows, issue, 0)
        pltpu.make_async_copy(hp_hbm.at[pl.ds(0, rows), :], o_ref, sem).wait()

    @pl.when(b >= nu_ref[0])
    def _():
        o_ref[...] = jnp.zeros(o_ref.shape, o_ref.dtype)


def _dispatch(n_used, buf_tok, hp):
    n_rows = buf_tok.shape[0]
    n_blk = n_rows // MOE_BLOCK
    w = hp.shape[1]
    return pl.pallas_call(
        _dispatch_kernel,
        grid_spec=pltpu.PrefetchScalarGridSpec(
            num_scalar_prefetch=1,
            grid=(n_blk,),
            in_specs=[pl.BlockSpec((1, 1, MOE_BLOCK), lambda b, nu: (b, 0, 0), memory_space=pltpu.SMEM),
                      pl.BlockSpec(memory_space=pl.ANY)],
            out_specs=pl.BlockSpec((MOE_BLOCK, w), lambda b, nu: (b, 0)),
            scratch_shapes=[pltpu.SemaphoreType.DMA(())]),
        out_shape=jax.ShapeDtypeStruct((n_rows, w), U32),
        compiler_params=_params("arbitrary"),
        name="moe_dispatch",
    )(n_used, buf_tok.reshape(n_blk, 1, MOE_BLOCK), hp)


def _expert_kernel(be_ref, nu_ref, xb_ref, wgu_ref, bgu_ref, wdn_ref, bdn_ref, o_ref):
    b = pl.program_id(0)

    @pl.when(b < nu_ref[0])
    def _():
        lo, hi = _unpack_pair(xb_ref[...])
        half = lo.shape[1]
        gu = (jnp.dot(lo.astype(BF16), wgu_ref[:half, :], preferred_element_type=F32)
              + jnp.dot(hi.astype(BF16), wgu_ref[half:, :], preferred_element_type=F32) + bgu_ref[...])
        g = jnp.minimum(gu[:, :D_EXPERT], SWIGLU_LIMIT)
        u = jnp.clip(gu[:, D_EXPERT:], -SWIGLU_LIMIT, SWIGLU_LIMIT)
        act = (u + 1.0) * (g * jax.nn.sigmoid(SWIGLU_ALPHA * g))
        out = jnp.dot(act.astype(BF16), wdn_ref[...], preferred_element_type=F32) + bdn_ref[...]
        o_ref[...] = _pack_pair(out[:, :half], out[:, half:])

    @pl.when(b >= nu_ref[0])
    def _():
        o_ref[...] = jnp.zeros(o_ref.shape, o_ref.dtype)


def _experts(block_e, n_used, xb, w_gu, b_gu, w_dn, b_dn):
    n_rows, w = xb.shape
    d = 2 * w
    n_blk = n_rows // MOE_BLOCK
    ne = w_gu.shape[0]
    return pl.pallas_call(
        _expert_kernel,
        grid_spec=pltpu.PrefetchScalarGridSpec(
            num_scalar_prefetch=2,
            grid=(n_blk,),
            in_specs=[pl.BlockSpec((MOE_BLOCK, w), lambda b, be, nu: (b, 0)),
                      pl.BlockSpec((None, d, 2 * D_EXPERT), lambda b, be, nu: (be[b], 0, 0)),
                      pl.BlockSpec((None, 1, 2 * D_EXPERT), lambda b, be, nu: (be[b], 0, 0)),
                      pl.BlockSpec((None, D_EXPERT, d), lambda b, be, nu: (be[b], 0, 0)),
                      pl.BlockSpec((None, 1, d), lambda b, be, nu: (be[b], 0, 0))],
            out_specs=pl.BlockSpec((MOE_BLOCK, w), lambda b, be, nu: (b, 0))),
        out_shape=jax.ShapeDtypeStruct((n_rows, w), U32),
        compiler_params=_params("arbitrary"),
        name="moe_experts",
    )(block_e, n_used, xb, w_gu, b_gu.reshape(ne, 1, -1), w_dn, b_dn.reshape(ne, 1, -1))


def _combine_kernel(dest_ref, tw_ref, x_ref, g2_ref, ng_ref, sc_ref, sh_ref, outp_hbm, *rest, bm, final):
    if final:
        ho_ref, buf, sem = rest
    else:
        xo_ref, ho_ref, buf, sem = rest

    def issue(r, carry):
        for k in range(TOP_K):
            dst = dest_ref[0, 0, r * TOP_K + k]
            pltpu.make_async_copy(outp_hbm.at[pl.ds(dst, 1), :], buf.at[k, pl.ds(r, 1), :], sem).start()
        return carry

    lax.fori_loop(0, bm, issue, 0)
    for k in range(TOP_K):
        pltpu.make_async_copy(outp_hbm.at[pl.ds(0, bm), :], buf.at[k], sem).wait()

    tw = tw_ref[...]
    y_lo = y_hi = None
    for k in range(TOP_K):
        lo, hi = _unpack_pair(buf[k])
        wk = tw[:, k:k + 1]
        y_lo = wk * lo if y_lo is None else y_lo + wk * lo
        y_hi = wk * hi if y_hi is None else y_hi + wk * hi
    x_new = x_ref[...] + g2_ref[...] * jnp.concatenate([y_lo, y_hi], axis=1)
    if final:
        ho_ref[...] = x_new * lax.rsqrt(jnp.mean(x_new * x_new, axis=-1, keepdims=True) + EPS) * ng_ref[...]
    else:
        xo_ref[...] = x_new
        ho_ref[...] = _rms_mod(x_new, ng_ref[...], sc_ref[...], sh_ref[...]).astype(BF16)


def _combine(dest, tw, x, g2, ng, sc, sh, outp, seq, final):
    t, d = x.shape
    bm = 256
    tps = seq // bm
    nt = t // bm
    row = lambda w: pl.BlockSpec((bm, w), lambda i: (i, 0))
    vec = pl.BlockSpec((None, 1, d), lambda i: (i // tps, 0, 0))
    if final:
        out_specs = [row(d)]
        out_shape = [jax.ShapeDtypeStruct((t, d), F32)]
    else:
        out_specs = [row(d), row(d)]
        out_shape = [jax.ShapeDtypeStruct((t, d), F32), jax.ShapeDtypeStruct((t, d), BF16)]
    return pl.pallas_call(
        functools.partial(_combine_kernel, bm=bm, final=final),
        grid=(nt,),
        in_specs=[pl.BlockSpec((1, 1, bm * TOP_K), lambda i: (i, 0, 0), memory_space=pltpu.SMEM),
                  row(LANES), row(d), vec, pl.BlockSpec((1, d), lambda i: (0, 0)), vec, vec,
                  pl.BlockSpec(memory_space=pl.ANY)],
        out_specs=out_specs,
        out_shape=out_shape,
        scratch_shapes=[pltpu.VMEM((TOP_K, bm, d // 2), U32), pltpu.SemaphoreType.DMA(())],
        compiler_params=_params("arbitrary"),
        name="moe_combine",
    )(dest.reshape(nt, 1, bm * TOP_K), tw, x, g2, ng, sc, sh, outp)


def _routing_tables(top_i):
    t = top_i.shape[0]
    n_asg = t * TOP_K
    flat_e = top_i.reshape(n_asg)
    onehot = (flat_e[:, None] == jnp.arange(N_EXPERTS, dtype=I32)[None, :]).astype(I32)
    csum = jnp.cumsum(onehot, axis=0)
    rank = jnp.take_along_axis(csum, flat_e[:, None], axis=1)[:, 0] - 1
    counts = csum[-1]
    padded = ((counts + MOE_BLOCK - 1) // MOE_BLOCK) * MOE_BLOCK
    pend = jnp.cumsum(padded)
    pstart = pend - padded
    dest = (pstart[flat_e] + rank).astype(I32)
    n_rows = ((n_asg + N_EXPERTS * MOE_BLOCK + MOE_BLOCK - 1) // MOE_BLOCK) * MOE_BLOCK
    n_blk = n_rows // MOE_BLOCK
    buf_tok = jnp.zeros((n_rows,), I32).at[dest].set(jnp.arange(n_asg, dtype=I32) // TOP_K)
    block_e = jnp.minimum(jnp.searchsorted(pend, jnp.arange(n_blk, dtype=I32) * MOE_BLOCK, side='right'),
                          N_EXPERTS - 1).astype(I32)
    n_used = (pend[-1:] // MOE_BLOCK).astype(I32)
    return dest, buf_tok, block_e, n_used


def _rotary_tables(positions):
    half = ROT_DIM // 2
    inv_freq = ROPE_THETA ** (-jnp.arange(0, ROT_DIM, 2, dtype=F32) / ROT_DIM)
    ang = positions.astype(F32).reshape(-1, 1) * inv_freq
    cos, sin = jnp.cos(ang), jnp.sin(ang)
    t = ang.shape[0]
    pad = jnp.zeros((t, DA_DK - ROT_DIM), F32)
    c64 = jnp.concatenate([cos, cos, pad + 1.0], axis=1)
    s1_64 = jnp.concatenate([jnp.zeros_like(sin), sin, pad], axis=1)
    s2_64 = jnp.concatenate([-sin, jnp.zeros_like(sin), pad], axis=1)
    rep = LANES // DA_DK
    return jnp.tile(c64, (1, rep)), jnp.tile(s1_64, (1, rep)), jnp.tile(s2_64, (1, rep))


def kernel(x, c, positions, w_ada, b_ada, norm_mix, norm_ffn, w_in, diff_lambda, diff_subln, gmlp_ln_g, gmlp_ln_b, gmlp_w_spatial, gmlp_b_spatial, pool_w, pool_scale, conv_w, w_branch, w_gate, b_gate, w_out, router_w, router_b, expert_w_gu, expert_b_gu, expert_w_down, expert_b_down, final_norm):
    bsz, seq, d = x.shape
    depth = w_ada.shape[0]
    t = bsz * seq
    assert seq % 512 == 0 and d == 2 * MIX_W

    rot_c, rot_s1, rot_s2 = _rotary_tables(positions)
    c_pad = jnp.zeros((8, d), F32).at[:bsz].set(c)
    mod = _adaln(c_pad, w_ada, b_ada)[:, :bsz].reshape(depth, bsz, 6, 1, d)
    rw_pad = jnp.zeros((depth, d, LANES), F32).at[:, :, :N_EXPERTS].set(router_w)
    rb_pad = jnp.full((depth, 1, LANES), NEG_INF, F32).at[:, 0, :N_EXPERTS].set(router_b)

    xf = x.reshape(t, d)
    h = None
    for l in range(depth):
        lam_init = 0.8 - 0.6 * math.exp(-0.3 * l)
        sh1, sc1, g1, sh2, sc2, g2 = (mod[l, :, i] for i in range(6))
        if l == 0:
            h = _norm_mod(xf, norm_mix[l].reshape(1, d), sc1, sh1, seq)
        z = _inproj(h, w_in[l].astype(BF16), rot_c, rot_s1, rot_s2)
        qt = z[:, COL_Q * MIX_W:(COL_Q + 1) * MIX_W].T
        vt = z[:, COL_V * MIX_W:(COL_V + 1) * MIX_W].T
        y_a = _attention(z, qt, vt, diff_lambda[l], diff_subln[l].reshape(DA_DV, 1), bsz, seq, lam_init)
        bs_b = jnp.broadcast_to(gmlp_b_spatial[l][:, :, None], (GM_GROUPS, GM_CHUNK, GM_CHUNK))
        y_b = _gmlp(z, gmlp_ln_g[l].reshape(1, MIX_W), gmlp_ln_b[l].reshape(1, MIX_W),
                    gmlp_w_spatial[l], bs_b)
        y_c = _pool(z, pool_w[l].astype(BF16), pool_scale[l].reshape(1, MIX_W), seq)
        y_d = _conv(z, conv_w[l], seq)
        merged = _merge(h, (y_a, y_b, y_c, y_d), w_gate[l].astype(BF16), b_gate[l], w_branch[l].astype(BF16))
        xf, hp, ti, tw = _outproj(merged, w_out[l].astype(BF16), xf, g1, norm_ffn[l].reshape(1, d), sc2, sh2,
                                  rw_pad[l], rb_pad[l], seq)
        dest, buf_tok, block_e, n_used = _routing_tables(ti[:, :TOP_K])
        xb = _dispatch(n_used, buf_tok, hp)
        outp = _experts(block_e, n_used, xb, expert_w_gu[l].astype(BF16), expert_b_gu[l],
                        expert_w_down[l].astype(BF16), expert_b_down[l])
        if l + 1 < depth:
            nsh1, nsc1 = mod[l + 1, :, 0], mod[l + 1, :, 1]
            xf, h = _combine(dest, tw, xf, g2, norm_mix[l + 1].reshape(1, d), nsc1, nsh1, outp, seq, False)
        else:
            zero = jnp.zeros((bsz, 1, d), F32)
            (out,) = _combine(dest, tw, xf, g2, final_norm.reshape(1, d), zero, zero, outp, seq, True)
    return out.reshape(bsz, seq, d)
```

```python
import functools
import math

import jax
import jax.numpy as jnp
from jax import lax
from jax.experimental import pallas as pl
from jax.experimental.pallas import tpu as pltpu

F32 = jnp.float32
BF16 = jnp.bfloat16
U32 = jnp.uint32
I32 = jnp.int32

MIX_W = 1024
DA_HEADS = 8
DA_DK = 64
DA_DV = 128
ROT_DIM = 16
ROPE_THETA = 500000.0
GM_GROUPS = 8
GM_CHUNK = 128
POOL_WINDOWS = (2, 4, 8, 16)
POOL_GC = 256
N_BRANCH = 4
N_EXPERTS = 32
TOP_K = 4
D_EXPERT = 1024
SWIGLU_LIMIT = 7.0
SWIGLU_ALPHA = 1.702
MOE_BLOCK = 512
EPS = 1e-6
NEG_INF = -1e30

LANES = 128
SUBLANES = 8
HALO = 16
VMEM_LIMIT = 56 * 1024 * 1024

COL_K, COL_GU, COL_GV, COL_POOL, COL_CB, COL_CC, COL_CH = range(7)
IN_Q, IN_K, IN_V = 0, 1, 2


def _params(*sem):
    return pltpu.CompilerParams(dimension_semantics=sem, vmem_limit_bytes=VMEM_LIMIT)


def _rms_mod(x, g, sc, sh):
    y = x * lax.rsqrt(jnp.mean(x * x, axis=-1, keepdims=True) + EPS) * g
    return y * (1.0 + sc) + sh


def _pack_pair(lo, hi):
    lo_bits = lax.bitcast_convert_type(lo.astype(BF16).astype(F32), U32) >> 16
    hi_bits = lax.bitcast_convert_type(hi.astype(BF16).astype(F32), U32) & jnp.uint32(0xFFFF0000)
    return hi_bits | lo_bits


def _unpack_pair(w):
    lo = lax.bitcast_convert_type(w << 16, F32)
    hi = lax.bitcast_convert_type(w & jnp.uint32(0xFFFF0000), F32)
    return lo, hi


def _store_row_tiles(ref, packed):
    n = packed.shape[0]
    for c in range(SUBLANES):
        ref[pl.ds(c, n, stride=SUBLANES), :] = packed[:, c * LANES:(c + 1) * LANES]


def _load_row_tile_col(ref, c, n):
    return ref[pl.ds(c, n, stride=SUBLANES), :]


def _adaln_kernel(c_ref, w_ref, b_ref, o_ref):
    c = c_ref[...]
    o_ref[...] = jnp.dot(c * jax.nn.sigmoid(c), w_ref[...], preferred_element_type=F32) + b_ref[...]


def _adaln(c_pad, w_ada, b_ada):
    depth, d, n = w_ada.shape
    bn = 1536
    rows = c_pad.shape[0]
    return pl.pallas_call(
        _adaln_kernel,
        grid=(depth, n // bn),
        in_specs=[pl.BlockSpec((rows, d), lambda l, j: (0, 0)),
                  pl.BlockSpec((None, d, bn), lambda l, j: (l, 0, j)),
                  pl.BlockSpec((None, 1, bn), lambda l, j: (l, 0, j))],
        out_specs=pl.BlockSpec((None, rows, bn), lambda l, j: (l, 0, j)),
        out_shape=jax.ShapeDtypeStruct((depth, rows, n), F32),
        compiler_params=_params("arbitrary", "arbitrary"),
        name="adaln",
    )(c_pad, w_ada, b_ada.reshape(depth, 1, n))


def _norm_kernel(x_ref, g_ref, sc_ref, sh_ref, h_ref):
    h_ref[...] = _rms_mod(x_ref[...], g_ref[...], sc_ref[...], sh_ref[...]).astype(BF16)


def _norm_mod(x, g, sc, sh, seq):
    t, d = x.shape
    bm = 512
    tps = seq // bm
    vec = pl.BlockSpec((None, 1, d), lambda i: (i // tps, 0, 0))
    return pl.pallas_call(
        _norm_kernel,
        grid=(t // bm,),
        in_specs=[pl.BlockSpec((bm, d), lambda i: (i, 0)),
                  pl.BlockSpec((1, d), lambda i: (0, 0)), vec, vec],
        out_specs=pl.BlockSpec((bm, d), lambda i: (i, 0)),
        out_shape=jax.ShapeDtypeStruct((t, d), BF16),
        compiler_params=_params("arbitrary"),
        name="norm_mod",
    )(x, g, sc, sh)


def _inproj_kernel(h_ref, w_ref, c_ref, s1_ref, s2_ref, z_ref, qt_ref, vt_ref):
    j = pl.program_id(1)
    acc = jnp.dot(h_ref[...], w_ref[...], preferred_element_type=F32)
    n_tiles = acc.shape[1] // LANES
    half = ROT_DIM // 2

    def rotary_tile(t, scale):
        xt = acc[:, t * LANES:(t + 1) * LANES]
        return (xt * (c_ref[...] * scale) + pltpu.roll(xt, half, 1) * (s1_ref[...] * scale)
                + pltpu.roll(xt, LANES - half, 1) * (s2_ref[...] * scale))

    @pl.when(j == IN_Q)
    def _():
        for t in range(n_tiles):
            qt_ref[t * LANES:(t + 1) * LANES, :] = rotary_tile(t, DA_DK ** -0.5 * math.log2(math.e)).T.astype(BF16)

    @pl.when(j == IN_K)
    def _():
        for t in range(n_tiles):
            z_ref[:, t * LANES:(t + 1) * LANES] = rotary_tile(t, 1.0).astype(BF16)

    @pl.when(j == IN_V)
    def _():
        for t in range(n_tiles):
            vt_ref[t * LANES:(t + 1) * LANES, :] = acc[:, t * LANES:(t + 1) * LANES].T.astype(BF16)

    @pl.when(j > IN_V)
    def _():
        z_ref[...] = acc.astype(BF16)


def _inproj(h, w_in, rot_c, rot_s1, rot_s2):
    t, d = h.shape
    n = w_in.shape[1]
    bm, bn = 1024, MIX_W
    nz = n // bn - 2
    rot = pl.BlockSpec((bm, LANES), lambda i, j: (i, 0))
    return pl.pallas_call(
        _inproj_kernel,
        grid=(t // bm, n // bn),
        in_specs=[pl.BlockSpec((bm, d), lambda i, j: (i, 0)),
                  pl.BlockSpec((d, bn), lambda i, j: (0, j)), rot, rot, rot],
        out_specs=[pl.BlockSpec((bm, bn), lambda i, j: (i, jnp.maximum(j - 2, 0))),
                   pl.BlockSpec((bn, bm), lambda i, j: (0, i)),
                   pl.BlockSpec((bn, bm), lambda i, j: (0, i))],
        out_shape=[jax.ShapeDtypeStruct((t, nz * bn), BF16),
                   jax.ShapeDtypeStruct((bn, t), BF16),
                   jax.ShapeDtypeStruct((bn, t), BF16)],
        compiler_params=_params("arbitrary", "arbitrary"),
        name="inproj",
    )(h, w_in, rot_c, rot_s1, rot_s2)


def _attn_kernel(qt_ref, k_ref, vt_ref, lam_ref, sub_ref, o_ref, sa_ref, sb_ref, m_sc, l_sc, acc_sc,
                 *, tq, lam_init):
    i = pl.program_id(2)
    qt = qt_ref[...]
    row = lax.broadcasted_iota(I32, qt.shape, 0)
    zero = jnp.zeros_like(qt)
    qs = jnp.concatenate([jnp.where(row < DA_DK, qt, zero), jnp.where(row >= DA_DK, qt, zero)], axis=1)
    m_sc[...] = jnp.full(m_sc.shape, NEG_INF, F32)
    l_sc[...] = jnp.zeros(l_sc.shape, F32)
    acc_sc[...] = jnp.zeros(acc_sc.shape, F32)

    def scores(j, s_ref):
        off = pl.multiple_of(j * tq, tq)
        s_ref[...] = jnp.dot(k_ref[pl.ds(off, tq), :], qs, preferred_element_type=F32)

    def softmax_pv(j, s_ref, masked):
        off = pl.multiple_of(j * tq, tq)
        s = s_ref[...]
        if masked:
            key = lax.broadcasted_iota(I32, s.shape, 0)
            c2 = lax.broadcasted_iota(I32, s.shape, 1)
            s = jnp.where(key <= jnp.where(c2 >= tq, c2 - tq, c2), s, NEG_INF)
        m_prev = m_sc[...]
        m_new = jnp.maximum(m_prev, jnp.max(s, axis=0, keepdims=True))
        alpha = jnp.exp2(m_prev - m_new)
        p = jnp.exp2(s - m_new)
        l_sc[...] = alpha * l_sc[...] + jnp.sum(p, axis=0, keepdims=True)
        acc_sc[...] = alpha * acc_sc[...] + jnp.dot(vt_ref[:, pl.ds(off, tq)], p.astype(BF16),
                                                    preferred_element_type=F32)
        m_sc[...] = m_new

    scores(0, sa_ref)

    def pair(jj, carry):
        j = 2 * jj
        scores(j + 1, sb_ref)
        softmax_pv(j, sa_ref, False)
        scores(j + 2, sa_ref)
        softmax_pv(j + 1, sb_ref, False)
        return carry

    lax.fori_loop(0, i // 2, pair, 0)

    @pl.when(i % 2 == 1)
    def _():
        scores(i, sb_ref)
        softmax_pv(i - 1, sa_ref, False)
        softmax_pv(i, sb_ref, True)

    @pl.when(i % 2 == 0)
    def _():
        softmax_pv(i, sa_ref, True)

    lv = lam_ref[...]
    lam = (jnp.exp(jnp.sum(lv[0:1] * lv[1:2], axis=-1, keepdims=True))
           - jnp.exp(jnp.sum(lv[2:3] * lv[3:4], axis=-1, keepdims=True)) + lam_init)
    acc = acc_sc[...]
    l = l_sc[...]
    o = acc[:, :tq] / l[:, :tq] - lam * (acc[:, tq:] / l[:, tq:])
    y = o * lax.rsqrt(jnp.mean(o * o, axis=0, keepdims=True) + EPS) * sub_ref[...] * (1.0 - lam_init)
    o_ref[...] = y.T.astype(o_ref.dtype)


def _attention(z, qt, vt, lam_vecs, subln_col, bsz, seq, lam_init):
    t = z.shape[0]
    tq = min(512, seq)
    nq = seq // tq
    return pl.pallas_call(
        functools.partial(_attn_kernel, tq=tq, lam_init=lam_init),
        grid=(bsz, DA_HEADS, nq),
        in_specs=[pl.BlockSpec((DA_DV, tq), lambda b, h, i: (h, b * nq + i)),
                  pl.BlockSpec((seq, DA_DV), lambda b, h, i: (b, COL_K * DA_HEADS + h)),
                  pl.BlockSpec((DA_DV, seq), lambda b, h, i: (h, b)),
                  pl.BlockSpec((4, DA_DK), lambda b, h, i: (0, 0)),
                  pl.BlockSpec((DA_DV, 1), lambda b, h, i: (0, 0))],
        out_specs=pl.BlockSpec((tq, DA_DV), lambda b, h, i: (b * nq + i, h)),
        out_shape=jax.ShapeDtypeStruct((t, MIX_W), BF16),
        scratch_shapes=[pltpu.VMEM((tq, 2 * tq), F32), pltpu.VMEM((tq, 2 * tq), F32),
                        pltpu.VMEM((1, 2 * tq), F32), pltpu.VMEM((1, 2 * tq), F32),
                        pltpu.VMEM((DA_DV, 2 * tq), F32)],
        compiler_params=_params("arbitrary", "arbitrary", "arbitrary"),
        name="diff_attn",
    )(qt, z, vt, lam_vecs, subln_col)


def _gelu(x):
    return 0.5 * x * (1.0 + lax.erf(x * math.sqrt(0.5)))


def _gmlp_kernel(u_ref, v_ref, g_ref, b_ref, ws_ref, bs_ref, o_ref, *, bm):
    v = _gelu(v_ref[...].astype(F32))
    mu = jnp.mean(v, axis=-1, keepdims=True)
    vc = v - mu
    var = jnp.mean(vc * vc, axis=-1, keepdims=True)
    vn = (vc * lax.rsqrt(var + EPS) * g_ref[...] + b_ref[...]).astype(BF16)
    r = lax.broadcasted_iota(I32, (GM_CHUNK, GM_CHUNK), 0)
    c = lax.broadcasted_iota(I32, (GM_CHUNK, GM_CHUNK), 1)
    for g in range(GM_GROUPS):
        cols = slice(g * GM_CHUNK, (g + 1) * GM_CHUNK)
        wm = jnp.where(c <= r, ws_ref[g], 0.0).astype(BF16)
        bias = bs_ref[g]
        for n in range(bm // GM_CHUNK):
            rows = slice(n * GM_CHUNK, (n + 1) * GM_CHUNK)
            sv = jnp.dot(wm, vn[rows, cols], preferred_element_type=F32) + bias
            u = _gelu(u_ref[rows, cols].astype(F32))
            o_ref[rows, cols] = (u * sv).astype(o_ref.dtype)


def _gmlp(z, ln_g, ln_b, ws, bs_b):
    t = z.shape[0]
    bm = 512
    full = lambda shape: pl.BlockSpec(shape, lambda i: (0,) * len(shape))
    return pl.pallas_call(
        functools.partial(_gmlp_kernel, bm=bm),
        grid=(t // bm,),
        in_specs=[pl.BlockSpec((bm, MIX_W), lambda i: (i, COL_GU)),
                  pl.BlockSpec((bm, MIX_W), lambda i: (i, COL_GV)),
                  full((1, MIX_W)), full((1, MIX_W)),
                  full((GM_GROUPS, GM_CHUNK, GM_CHUNK)), full((GM_GROUPS, GM_CHUNK, GM_CHUNK))],
        out_specs=pl.BlockSpec((bm, MIX_W), lambda i: (i, 0)),
        out_shape=jax.ShapeDtypeStruct((t, MIX_W), BF16),
        compiler_params=_params("arbitrary"),
        name="gmlp",
    )(z, z, ln_g, ln_b, ws, bs_b)


def _pool_kernel(z_ref, halo_ref, pw_ref, ps_ref, o_ref, *, bm, tps):
    p0 = (pl.program_id(0) % tps) * bm
    d = (lax.broadcasted_iota(I32, (bm, bm), 0) - lax.broadcasted_iota(I32, (bm, bm), 1))
    dh = (lax.broadcasted_iota(I32, (bm, HALO), 0) + HALO - lax.broadcasted_iota(I32, (bm, HALO), 1))
    pos = p0 + lax.broadcasted_iota(I32, (bm, 1), 0)
    for g, w in enumerate(POOL_WINDOWS):
        cols = slice(g * POOL_GC, (g + 1) * POOL_GC)
        x = z_ref[:, cols]
        hx = halo_ref[:, cols]
        hx = jnp.where(p0 > 0, hx, jnp.zeros_like(hx))
        band = jnp.where(d >= 0, jnp.where(d < w, 1.0, 0.0), 0.0).astype(BF16)
        band_h = jnp.where(dh < w, 1.0, 0.0).astype(BF16)
        win = (jnp.dot(band, x, preferred_element_type=F32)
               + jnp.dot(band_h, hx, preferred_element_type=F32))
        cnt = jnp.minimum(pos + 1, w).astype(F32)
        pooled = (win / cnt - x.astype(F32)).astype(BF16)
        mixed = jnp.dot(pooled, pw_ref[g], preferred_element_type=F32)
        o_ref[:, cols] = (mixed * ps_ref[:, cols]).astype(o_ref.dtype)


def _halo_spec(bm, col):
    return pl.BlockSpec((HALO, MIX_W), lambda i: (jnp.maximum(i * (bm // HALO) - 1, 0), col))


def _pool(z, pw, ps, seq):
    t = z.shape[0]
    bm = min(512, seq)
    ng = len(POOL_WINDOWS)
    return pl.pallas_call(
        functools.partial(_pool_kernel, bm=bm, tps=seq // bm),
        grid=(t // bm,),
        in_specs=[pl.BlockSpec((bm, MIX_W), lambda i: (i, COL_POOL)),
                  _halo_spec(bm, COL_POOL),
                  pl.BlockSpec((ng, POOL_GC, POOL_GC), lambda i: (0, 0, 0)),
                  pl.BlockSpec((1, MIX_W), lambda i: (0, 0))],
        out_specs=pl.BlockSpec((bm, MIX_W), lambda i: (i, 0)),
        out_shape=jax.ShapeDtypeStruct((t, MIX_W), BF16),
        compiler_params=_params("arbitrary"),
        name="pool",
    )(z, z, pw, ps)


def _conv_kernel(b_ref, c_ref, h_ref, hc_ref, hh_ref, cw_ref, o_ref, *, tps):
    first = (pl.program_id(0) % tps) == 0
    xin = c_ref[...].astype(F32) * h_ref[...].astype(F32)
    hal = hc_ref[...].astype(F32) * hh_ref[...].astype(F32)
    hal = jnp.where(first, 0.0, hal)
    hm1 = hal[HALO - 1:HALO, :]
    hm2 = hal[HALO - 2:HALO - 1, :]
    row = lax.broadcasted_iota(I32, xin.shape, 0)
    s1 = jnp.where(row == 0, hm1, pltpu.roll(xin, 1, 0))
    s2 = jnp.where(row == 0, hm2, jnp.where(row == 1, hm1, pltpu.roll(xin, 2, 0)))
    cw = cw_ref[...]
    y = cw[0:1] * s2 + cw[1:2] * s1 + cw[2:3] * xin
    o_ref[...] = (b_ref[...].astype(F32) * y).astype(o_ref.dtype)


def _conv(z, cw, seq):
    t = z.shape[0]
    bm = min(512, seq)
    blk = lambda col: pl.BlockSpec((bm, MIX_W), lambda i: (i, col))
    return pl.pallas_call(
        functools.partial(_conv_kernel, tps=seq // bm),
        grid=(t // bm,),
        in_specs=[blk(COL_CB), blk(COL_CC), blk(COL_CH), _halo_spec(bm, COL_CC), _halo_spec(bm, COL_CH),
                  pl.BlockSpec(cw.shape, lambda i: (0, 0))],
        out_specs=pl.BlockSpec((bm, MIX_W), lambda i: (i, 0)),
        out_shape=jax.ShapeDtypeStruct((t, MIX_W), BF16),
        compiler_params=_params("arbitrary"),
        name="short_conv",
    )(z, z, z, z, z, cw)


def _merge_kernel(h_ref, ya_ref, yb_ref, yc_ref, yd_ref, wg_ref, bg_ref, wb_ref, o_ref):
    h = h_ref[...]
    bg = bg_ref[...]
    merged = None
    for i, y_ref in enumerate((ya_ref, yb_ref, yc_ref, yd_ref)):
        gate = jax.nn.sigmoid(jnp.dot(h, wg_ref[i], preferred_element_type=F32) + bg[i:i + 1])
        term = gate * jnp.dot(y_ref[...], wb_ref[i], preferred_element_type=F32)
        merged = term if merged is None else merged + term
    o_ref[...] = merged.astype(o_ref.dtype)


def _merge(h, ys, wg, bg, wb):
    t, d = h.shape
    bm, bn = 1024, 256
    yspec = pl.BlockSpec((bm, MIX_W), lambda i, j: (i, 0))
    return pl.pallas_call(
        _merge_kernel,
        grid=(t // bm, d // bn),
        in_specs=[pl.BlockSpec((bm, d), lambda i, j: (i, 0)), yspec, yspec, yspec, yspec,
                  pl.BlockSpec((N_BRANCH, d, bn), lambda i, j: (0, 0, j)),
                  pl.BlockSpec((N_BRANCH, bn), lambda i, j: (0, j)),
                  pl.BlockSpec((N_BRANCH, MIX_W, bn), lambda i, j: (0, 0, j))],
        out_specs=pl.BlockSpec((bm, bn), lambda i, j: (i, j)),
        out_shape=jax.ShapeDtypeStruct((t, d), BF16),
        compiler_params=_params("arbitrary", "arbitrary"),
        name="branch_merge",
    )(h, *ys, wg, bg, wb)


def _outproj_kernel(m_ref, wo_ref, x_ref, g1_ref, ng_ref, sc_ref, sh_ref, rw_ref, rb_ref,
                    xo_ref, hp_ref, ti_ref, tw_ref, tr_ref, cnt_ref, carry_sc):
    out = jnp.dot(m_ref[...], wo_ref[...], preferred_element_type=F32)
    x_new = x_ref[...] + g1_ref[...] * out
    xo_ref[...] = x_new
    h2 = _rms_mod(x_new, ng_ref[...], sc_ref[...], sh_ref[...])
    half = h2.shape[1] // 2
    _store_row_tiles(hp_ref, _pack_pair(h2[:, :half], h2[:, half:]))

    logits = jnp.dot(h2, rw_ref[...], preferred_element_type=F32) + rb_ref[...]
    lane = lax.broadcasted_iota(I32, logits.shape, 1).astype(F32)
    vals, idxs = [], []
    cur = logits
    for _ in range(TOP_K):
        m = jnp.max(cur, axis=-1, keepdims=True)
        idx = jnp.min(jnp.where(cur == m, lane, float(LANES)), axis=-1, keepdims=True)
        vals.append(m)
        idxs.append(idx)
        cur = jnp.where(lane == idx, -3e38, cur)
    es = [jnp.exp(v - vals[0]) for v in vals]
    den = es[0] + es[1] + es[2] + es[3]

    @pl.when(pl.program_id(0) == 0)
    def _():
        carry_sc[...] = jnp.zeros(carry_sc.shape, F32)

    onehot = jnp.zeros(logits.shape, F32)
    for k in range(TOP_K):
        onehot = onehot + jnp.where(lane == idxs[k], 1.0, 0.0)
    bm = logits.shape[0]
    tri = jnp.where(lax.broadcasted_iota(I32, (bm, bm), 0) > lax.broadcasted_iota(I32, (bm, bm), 1),
                    1.0, 0.0).astype(BF16)
    before = jnp.dot(tri, onehot.astype(BF16), preferred_element_type=F32) + carry_sc[...]
    carry = carry_sc[...] + jnp.sum(onehot, axis=0, keepdims=True)
    carry_sc[...] = carry
    cnt_ref[...] = jnp.broadcast_to(carry, cnt_ref.shape)

    ti = jnp.zeros(logits.shape, F32)
    tw = jnp.zeros(logits.shape, F32)
    tr = jnp.zeros(logits.shape, F32)
    for k in range(TOP_K):
        rank = jnp.sum(jnp.where(lane == idxs[k], before, 0.0), axis=-1, keepdims=True)
        ti = jnp.where(lane == float(k), idxs[k], ti)
        tw = jnp.where(lane == float(k), es[k] / den, tw)
        tr = jnp.where(lane == float(k), rank, tr)
    ti_ref[...] = ti.astype(I32)
    tw_ref[...] = tw
    tr_ref[...] = tr.astype(I32)


def _outproj(merged, wo, x, g1, ng, sc, sh, rw, rb, seq):
    t, d = x.shape
    bm = 512
    tps = seq // bm
    row = lambda w: pl.BlockSpec((bm, w), lambda i: (i, 0))
    vec = pl.BlockSpec((None, 1, d), lambda i: (i // tps, 0, 0))
    full = lambda a: pl.BlockSpec(a.shape, lambda i: (0, 0))
    return pl.pallas_call(
        _outproj_kernel,
        grid=(t // bm,),
        in_specs=[row(d), full(wo), row(d), vec, full(ng), vec, vec, full(rw), full(rb)],
        out_specs=[row(d), pl.BlockSpec((bm * SUBLANES, LANES), lambda i: (i, 0)),
                   row(LANES), row(LANES), row(LANES),
                   pl.BlockSpec((SUBLANES, LANES), lambda i: (0, 0))],
        out_shape=[jax.ShapeDtypeStruct((t, d), F32), jax.ShapeDtypeStruct((t * SUBLANES, LANES), U32),
                   jax.ShapeDtypeStruct((t, LANES), I32), jax.ShapeDtypeStruct((t, LANES), F32),
                   jax.ShapeDtypeStruct((t, LANES), I32), jax.ShapeDtypeStruct((SUBLANES, LANES), F32)],
        scratch_shapes=[pltpu.VMEM((1, LANES), F32)],
        compiler_params=_params("arbitrary"),
        name="outproj_router",
    )(merged, wo, x, g1, ng, sc, sh, rw, rb)


def _row_copy(src_ref, src_row, dst_ref, dst_row, sem):
    return pltpu.make_async_copy(src_ref.at[pl.ds(pl.multiple_of(src_row * SUBLANES, SUBLANES), SUBLANES), :],
                                 dst_ref.at[pl.ds(pl.multiple_of(dst_row * SUBLANES, SUBLANES), SUBLANES), :],
                                 sem)


def _dispatch_kernel(dest_ref, hp_ref, init_hbm, xb_hbm, sem, *, bm):
    del init_hbm

    def issue(r, carry):
        for k in range(TOP_K):
            _row_copy(hp_ref, r, xb_hbm, dest_ref[0, 0, r * TOP_K + k], sem).start()
        return carry

    lax.fori_loop(0, bm, issue, 0, unroll=4)
    for k in range(TOP_K):
        pltpu.make_async_copy(hp_ref, xb_hbm.at[pl.ds(0, bm * SUBLANES), :], sem).wait()


def _dispatch(dest, hp, n_rows):
    t = dest.shape[0]
    bm = 512
    nt = t // bm
    xb_init = jnp.zeros((n_rows * SUBLANES, LANES), U32)
    return pl.pallas_call(
        functools.partial(_dispatch_kernel, bm=bm),
        grid=(nt,),
        in_specs=[pl.BlockSpec((1, 1, bm * TOP_K), lambda i: (i, 0, 0), memory_space=pltpu.SMEM),
                  pl.BlockSpec((bm * SUBLANES, LANES), lambda i: (i, 0)),
                  pl.BlockSpec(memory_space=pl.ANY)],
        out_specs=pl.BlockSpec(memory_space=pl.ANY),
        out_shape=jax.ShapeDtypeStruct((n_rows * SUBLANES, LANES), U32),
        scratch_shapes=[pltpu.SemaphoreType.DMA(())],
        input_output_aliases={2: 0},
        compiler_params=_params("arbitrary"),
        name="moe_dispatch",
    )(dest.reshape(nt, 1, bm * TOP_K), hp, xb_init)


def _expert_kernel(be_ref, nu_ref, xb_ref, wgu_ref, bgu_ref, wdn_ref, bdn_ref, o_ref):
    b = pl.program_id(0)
    rows = MOE_BLOCK

    @pl.when(b < nu_ref[0])
    def _():
        los, his = [], []
        for c in range(SUBLANES):
            lo, hi = _unpack_pair(_load_row_tile_col(xb_ref, c, rows))
            los.append(lo.astype(BF16))
            his.append(hi.astype(BF16))
        lo = jnp.concatenate(los, axis=1)
        hi = jnp.concatenate(his, axis=1)
        half = lo.shape[1]
        gu = (jnp.dot(lo, wgu_ref[:half, :], preferred_element_type=F32)
              + jnp.dot(hi, wgu_ref[half:, :], preferred_element_type=F32) + bgu_ref[...])
        g = jnp.minimum(gu[:, :D_EXPERT], SWIGLU_LIMIT)
        u = jnp.clip(gu[:, D_EXPERT:], -SWIGLU_LIMIT, SWIGLU_LIMIT)
        act = (u + 1.0) * (g * jax.nn.sigmoid(SWIGLU_ALPHA * g))
        out = jnp.dot(act.astype(BF16), wdn_ref[...], preferred_element_type=F32) + bdn_ref[...]
        _store_row_tiles(o_ref, _pack_pair(out[:, :half], out[:, half:]))

    @pl.when(b >= nu_ref[0])
    def _():
        o_ref[...] = jnp.zeros(o_ref.shape, o_ref.dtype)


def _experts(block_e, n_used, xb, w_gu, b_gu, w_dn, b_dn):
    n_blk = xb.shape[0] // (MOE_BLOCK * SUBLANES)
    ne, d, _ = w_gu.shape
    blk = pl.BlockSpec((MOE_BLOCK * SUBLANES, LANES), lambda b, be, nu: (b, 0))
    blk_in = pl.BlockSpec((MOE_BLOCK * SUBLANES, LANES), lambda b, be, nu: (jnp.minimum(b, nu[0] - 1), 0))
    return pl.pallas_call(
        _expert_kernel,
        grid_spec=pltpu.PrefetchScalarGridSpec(
            num_scalar_prefetch=2,
            grid=(n_blk,),
            in_specs=[blk_in,
                      pl.BlockSpec((None, d, 2 * D_EXPERT), lambda b, be, nu: (be[b], 0, 0)),
                      pl.BlockSpec((None, 1, 2 * D_EXPERT), lambda b, be, nu: (be[b], 0, 0)),
                      pl.BlockSpec((None, D_EXPERT, d), lambda b, be, nu: (be[b], 0, 0)),
                      pl.BlockSpec((None, 1, d), lambda b, be, nu: (be[b], 0, 0))],
            out_specs=blk),
        out_shape=jax.ShapeDtypeStruct(xb.shape, U32),
        compiler_params=_params("arbitrary"),
        name="moe_experts",
    )(block_e, n_used, xb, w_gu, b_gu.reshape(ne, 1, -1), w_dn, b_dn.reshape(ne, 1, -1))


def _combine_kernel(dest_ref, tw_ref, x_ref, g2_ref, ng_ref, sc_ref, sh_ref, outp_hbm, *rest, bm, final):
    if final:
        ho_ref, buf, sem = rest
    else:
        xo_ref, ho_ref, buf, sem = rest

    def issue(r, carry):
        for k in range(TOP_K):
            _row_copy(outp_hbm, dest_ref[0, 0, r * TOP_K + k], buf.at[k], r, sem).start()
        return carry

    lax.fori_loop(0, bm, issue, 0, unroll=4)
    for k in range(TOP_K):
        pltpu.make_async_copy(outp_hbm.at[pl.ds(0, bm * SUBLANES), :], buf.at[k], sem).wait()

    tw = tw_ref[...]
    wk = [tw[:, k:k + 1] for k in range(TOP_K)]
    y_lo, y_hi = [], []
    for c in range(SUBLANES):
        a_lo = a_hi = None
        for k in range(TOP_K):
            lo, hi = _unpack_pair(buf[k, pl.ds(c, bm, stride=SUBLANES), :])
            a_lo = wk[k] * lo if a_lo is None else a_lo + wk[k] * lo
            a_hi = wk[k] * hi if a_hi is None else a_hi + wk[k] * hi
        y_lo.append(a_lo)
        y_hi.append(a_hi)
    x_new = x_ref[...] + g2_ref[...] * jnp.concatenate(y_lo + y_hi, axis=1)
    if final:
        ho_ref[...] = x_new * lax.rsqrt(jnp.mean(x_new * x_new, axis=-1, keepdims=True) + EPS) * ng_ref[...]
    else:
        xo_ref[...] = x_new
        ho_ref[...] = _rms_mod(x_new, ng_ref[...], sc_ref[...], sh_ref[...]).astype(BF16)


def _combine(dest, tw, x, g2, ng, sc, sh, outp, seq, final):
    t, d = x.shape
    bm = 256
    tps = seq // bm
    nt = t // bm
    row = lambda w: pl.BlockSpec((bm, w), lambda i: (i, 0))
    vec = pl.BlockSpec((None, 1, d), lambda i: (i // tps, 0, 0))
    if final:
        out_specs = [row(d)]
        out_shape = [jax.ShapeDtypeStruct((t, d), F32)]
    else:
        out_specs = [row(d), row(d)]
        out_shape = [jax.ShapeDtypeStruct((t, d), F32), jax.ShapeDtypeStruct((t, d), BF16)]
    return pl.pallas_call(
        functools.partial(_combine_kernel, bm=bm, final=final),
        grid=(nt,),
        in_specs=[pl.BlockSpec((1, 1, bm * TOP_K), lambda i: (i, 0, 0), memory_space=pltpu.SMEM),
                  row(LANES), row(d), vec, pl.BlockSpec((1, d), lambda i: (0, 0)), vec, vec,
                  pl.BlockSpec(memory_space=pl.ANY)],
        out_specs=out_specs,
        out_shape=out_shape,
        scratch_shapes=[pltpu.VMEM((TOP_K, bm * SUBLANES, LANES), U32), pltpu.SemaphoreType.DMA(())],
        compiler_params=_params("arbitrary"),
        name="moe_combine",
    )(dest.reshape(nt, 1, bm * TOP_K), tw, x, g2, ng, sc, sh, outp)


def _routing_tables(top_i, rank, counts_f):
    t = top_i.shape[0]
    n_asg = t * TOP_K
    counts = counts_f.astype(I32)
    padded = ((counts + MOE_BLOCK - 1) // MOE_BLOCK) * MOE_BLOCK
    pend = jnp.cumsum(padded)
    pstart = pend - padded
    experts = jnp.arange(N_EXPERTS, dtype=I32)
    start_of = jnp.sum(jnp.where(top_i[:, :, None] == experts, pstart, 0), axis=-1)
    dest = (start_of + rank).astype(I32)
    n_rows = ((n_asg + N_EXPERTS * MOE_BLOCK + MOE_BLOCK - 1) // MOE_BLOCK) * MOE_BLOCK
    n_blk = n_rows // MOE_BLOCK
    block_e = jnp.minimum(jnp.searchsorted(pend, jnp.arange(n_blk, dtype=I32) * MOE_BLOCK, side='right'),
                          N_EXPERTS - 1).astype(I32)
    n_used = (pend[-1:] // MOE_BLOCK).astype(I32)
    return dest, block_e, n_used, n_rows


def _rotary_tables(positions):
    inv_freq = ROPE_THETA ** (-jnp.arange(0, ROT_DIM, 2, dtype=F32) / ROT_DIM)
    ang = positions.astype(F32).reshape(-1, 1) * inv_freq
    cos, sin = jnp.cos(ang), jnp.sin(ang)
    t = ang.shape[0]
    pad = jnp.zeros((t, DA_DK - ROT_DIM), F32)
    c64 = jnp.concatenate([cos, cos, pad + 1.0], axis=1)
    s1_64 = jnp.concatenate([jnp.zeros_like(sin), sin, pad], axis=1)
    s2_64 = jnp.concatenate([-sin, jnp.zeros_like(sin), pad], axis=1)
    rep = LANES // DA_DK
    return jnp.tile(c64, (1, rep)), jnp.tile(s1_64, (1, rep)), jnp.tile(s2_64, (1, rep))


def kernel(x, c, positions, w_ada, b_ada, norm_mix, norm_ffn, w_in, diff_lambda, diff_subln, gmlp_ln_g, gmlp_ln_b, gmlp_w_spatial, gmlp_b_spatial, pool_w, pool_scale, conv_w, w_branch, w_gate, b_gate, w_out, router_w, router_b, expert_w_gu, expert_b_gu, expert_w_down, expert_b_down, final_norm):
    bsz, seq, d = x.shape
    depth = w_ada.shape[0]
    t = bsz * seq
    assert seq % 512 == 0 and d == 2 * MIX_W and d // 2 == SUBLANES * LANES

    rot_c, rot_s1, rot_s2 = _rotary_tables(positions)
    c_pad = jnp.zeros((SUBLANES, d), F32).at[:bsz].set(c)
    mod = _adaln(c_pad, w_ada, b_ada)[:, :bsz].reshape(depth, bsz, 6, 1, d)
    rw_pad = jnp.zeros((depth, d, LANES), F32).at[:, :, :N_EXPERTS].set(router_w)
    rb_pad = jnp.full((depth, 1, LANES), NEG_INF, F32).at[:, 0, :N_EXPERTS].set(router_b)

    xf = x.reshape(t, d)
    h = None
    for l in range(depth):
        lam_init = 0.8 - 0.6 * math.exp(-0.3 * l)
        sh1, sc1, g1, sh2, sc2, g2 = (mod[l, :, i] for i in range(6))
        if l == 0:
            h = _norm_mod(xf, norm_mix[l].reshape(1, d), sc1, sh1, seq)
        z, qt, vt = _inproj(h, w_in[l].astype(BF16), rot_c, rot_s1, rot_s2)
        y_a = _attention(z, qt, vt, diff_lambda[l], diff_subln[l].reshape(DA_DV, 1), bsz, seq, lam_init)
        bs_b = jnp.broadcast_to(gmlp_b_spatial[l][:, :, None], (GM_GROUPS, GM_CHUNK, GM_CHUNK))
        y_b = _gmlp(z, gmlp_ln_g[l].reshape(1, MIX_W), gmlp_ln_b[l].reshape(1, MIX_W),
                    gmlp_w_spatial[l], bs_b)
        y_c = _pool(z, pool_w[l].astype(BF16), pool_scale[l].reshape(1, MIX_W), seq)
        y_d = _conv(z, conv_w[l], seq)
        merged = _merge(h, (y_a, y_b, y_c, y_d), w_gate[l].astype(BF16), b_gate[l], w_branch[l].astype(BF16))
        xf, hp, ti, tw, tr, cnt = _outproj(merged, w_out[l].astype(BF16), xf, g1, norm_ffn[l].reshape(1, d),
                                           sc2, sh2, rw_pad[l], rb_pad[l], seq)
        dest, block_e, n_used, n_rows = _routing_tables(ti[:, :TOP_K], tr[:, :TOP_K], cnt[0, :N_EXPERTS])
        xb = _dispatch(dest, hp, n_rows)
        outp = _experts(block_e, n_used, xb, expert_w_gu[l].astype(BF16), expert_b_gu[l],
                        expert_w_down[l].astype(BF16), expert_b_down[l])
        if l + 1 < depth:
            nsh1, nsc1 = mod[l + 1, :, 0], mod[l + 1, :, 1]
            xf, h = _combine(dest, tw, xf, g2, norm_mix[l + 1].reshape(1, d), nsc1, nsh1, outp, seq, False)
        else:
            zero = jnp.zeros((bsz, 1, d), F32)
            (out,) = _combine(dest, tw, xf, g2, final_norm.reshape(1, d), zero, zero, outp, seq, True)
    return out.reshape(bsz, seq, d)
```

```python
import functools
import math

import jax
import jax.numpy as jnp
from jax import lax
from jax.experimental import pallas as pl
from jax.experimental.pallas import tpu as pltpu

F32 = jnp.float32
BF16 = jnp.bfloat16
U32 = jnp.uint32
I32 = jnp.int32

MIX_W = 1024
DA_HEADS = 8
DA_DK = 64
DA_DV = 128
ROT_DIM = 16
ROPE_THETA = 500000.0
GM_GROUPS = 8
GM_CHUNK = 128
POOL_WINDOWS = (2, 4, 8, 16)
POOL_GC = 256
N_BRANCH = 4
N_EXPERTS = 32
TOP_K = 4
D_EXPERT = 1024
SWIGLU_LIMIT = 7.0
SWIGLU_ALPHA = 1.702
MOE_BLOCK = 512
EPS = 1e-6
NEG_INF = -1e30

LANES = 128
SUBLANES = 8
HALO = 16
ONES_ROWS = 16
VMEM_LIMIT = 56 * 1024 * 1024

COL_K, COL_GU, COL_GV, COL_POOL, COL_CB, COL_CC, COL_CH = range(7)
IN_Q, IN_K, IN_V = 0, 1, 2


def _params(*sem):
    return pltpu.CompilerParams(dimension_semantics=sem, vmem_limit_bytes=VMEM_LIMIT)


def _rms_mod(x, g, sc, sh):
    y = x * lax.rsqrt(jnp.mean(x * x, axis=-1, keepdims=True) + EPS) * g
    return y * (1.0 + sc) + sh


def _pack_pair(lo, hi):
    lo_bits = lax.bitcast_convert_type(lo.astype(BF16).astype(F32), U32) >> 16
    hi_bits = lax.bitcast_convert_type(hi.astype(BF16).astype(F32), U32) & jnp.uint32(0xFFFF0000)
    return hi_bits | lo_bits


def _unpack_pair(w):
    lo = lax.bitcast_convert_type(w << 16, F32)
    hi = lax.bitcast_convert_type(w & jnp.uint32(0xFFFF0000), F32)
    return lo, hi


def _store_row_tiles(ref, packed):
    n = packed.shape[0]
    for c in range(SUBLANES):
        ref[pl.ds(c, n, stride=SUBLANES), :] = packed[:, c * LANES:(c + 1) * LANES]


def _load_row_tile_col(ref, c, n):
    return ref[pl.ds(c, n, stride=SUBLANES), :]


def _adaln_kernel(c_ref, w_ref, b_ref, o_ref):
    c = c_ref[...]
    o_ref[...] = jnp.dot(c * jax.nn.sigmoid(c), w_ref[...], preferred_element_type=F32) + b_ref[...]


def _adaln(c_pad, w_ada, b_ada):
    depth, d, n = w_ada.shape
    bn = 1536
    rows = c_pad.shape[0]
    return pl.pallas_call(
        _adaln_kernel,
        grid=(depth, n // bn),
        in_specs=[pl.BlockSpec((rows, d), lambda l, j: (0, 0)),
                  pl.BlockSpec((None, d, bn), lambda l, j: (l, 0, j)),
                  pl.BlockSpec((None, 1, bn), lambda l, j: (l, 0, j))],
        out_specs=pl.BlockSpec((None, rows, bn), lambda l, j: (l, 0, j)),
        out_shape=jax.ShapeDtypeStruct((depth, rows, n), F32),
        compiler_params=_params("arbitrary", "arbitrary"),
        name="adaln",
    )(c_pad, w_ada, b_ada.reshape(depth, 1, n))


def _norm_kernel(x_ref, g_ref, sc_ref, sh_ref, h_ref):
    h_ref[...] = _rms_mod(x_ref[...], g_ref[...], sc_ref[...], sh_ref[...]).astype(BF16)


def _norm_mod(x, g, sc, sh, seq):
    t, d = x.shape
    bm = 512
    tps = seq // bm
    vec = pl.BlockSpec((None, 1, d), lambda i: (i // tps, 0, 0))
    return pl.pallas_call(
        _norm_kernel,
        grid=(t // bm,),
        in_specs=[pl.BlockSpec((bm, d), lambda i: (i, 0)),
                  pl.BlockSpec((1, d), lambda i: (0, 0)), vec, vec],
        out_specs=pl.BlockSpec((bm, d), lambda i: (i, 0)),
        out_shape=jax.ShapeDtypeStruct((t, d), BF16),
        compiler_params=_params("arbitrary"),
        name="norm_mod",
    )(x, g, sc, sh)


def _inproj_kernel(h_ref, w_ref, c_ref, s1_ref, s2_ref, z_ref, qt_ref, vt_ref):
    j = pl.program_id(1)
    acc = jnp.dot(h_ref[...], w_ref[...], preferred_element_type=F32)
    n_tiles = acc.shape[1] // LANES
    half = ROT_DIM // 2

    def rotary_tile(t, scale):
        xt = acc[:, t * LANES:(t + 1) * LANES]
        return (xt * (c_ref[...] * scale) + pltpu.roll(xt, half, 1) * (s1_ref[...] * scale)
                + pltpu.roll(xt, LANES - half, 1) * (s2_ref[...] * scale))

    @pl.when(j == IN_Q)
    def _():
        for t in range(n_tiles):
            qt_ref[t * LANES:(t + 1) * LANES, :] = rotary_tile(t, DA_DK ** -0.5 * math.log2(math.e)).T.astype(BF16)

    @pl.when(j == IN_K)
    def _():
        for t in range(n_tiles):
            z_ref[:, t * LANES:(t + 1) * LANES] = rotary_tile(t, 1.0).astype(BF16)

    @pl.when(j == IN_V)
    def _():
        for t in range(n_tiles):
            vt_ref[t * LANES:(t + 1) * LANES, :] = acc[:, t * LANES:(t + 1) * LANES].T.astype(BF16)

    @pl.when(j > IN_V)
    def _():
        z_ref[...] = acc.astype(BF16)


def _inproj(h, w_in, layer, rot_c, rot_s1, rot_s2):
    t, d = h.shape
    n = w_in.shape[2]
    bm, bn = 1024, MIX_W
    nz = n // bn - 2
    rot = pl.BlockSpec((bm, LANES), lambda i, j: (i, 0))
    return pl.pallas_call(
        _inproj_kernel,
        grid=(t // bm, n // bn),
        in_specs=[pl.BlockSpec((bm, d), lambda i, j: (i, 0)),
                  pl.BlockSpec((None, d, bn), lambda i, j: (layer, 0, j)), rot, rot, rot],
        out_specs=[pl.BlockSpec((bm, bn), lambda i, j: (i, jnp.maximum(j - 2, 0))),
                   pl.BlockSpec((bn, bm), lambda i, j: (0, i)),
                   pl.BlockSpec((bn, bm), lambda i, j: (0, i))],
        out_shape=[jax.ShapeDtypeStruct((t, nz * bn), BF16),
                   jax.ShapeDtypeStruct((bn, t), BF16),
                   jax.ShapeDtypeStruct((bn, t), BF16)],
        compiler_params=_params("arbitrary", "arbitrary"),
        name="inproj",
    )(h, w_in, rot_c, rot_s1, rot_s2)


def _attn_kernel(qt_ref, k_ref, vt_ref, lam_ref, sub_ref, o_ref, sa_ref, sb_ref, m_sc, acc_sc,
                 *, tq, lam_init):
    i = pl.program_id(2)
    qt = qt_ref[...]
    row = lax.broadcasted_iota(I32, qt.shape, 0)
    zero = jnp.zeros_like(qt)
    qs = jnp.concatenate([jnp.where(row < DA_DK, qt, zero), jnp.where(row >= DA_DK, qt, zero)], axis=1)
    m_sc[...] = jnp.full(m_sc.shape, NEG_INF, F32)
    acc_sc[...] = jnp.zeros(acc_sc.shape, F32)

    def scores(j, s_ref):
        off = pl.multiple_of(j * tq, tq)
        s_ref[...] = jnp.dot(k_ref[pl.ds(off, tq), :], qs, preferred_element_type=F32)

    def softmax_pv(j, s_ref, masked):
        off = pl.multiple_of(j * tq, tq)
        s = s_ref[...]
        if masked:
            key = lax.broadcasted_iota(I32, s.shape, 0)
            c2 = lax.broadcasted_iota(I32, s.shape, 1)
            s = jnp.where(key <= jnp.where(c2 >= tq, c2 - tq, c2), s, NEG_INF)
        m_prev = m_sc[...]
        m_new = jnp.maximum(m_prev, jnp.max(s, axis=0, keepdims=True))
        alpha = jnp.exp2(m_prev - m_new)
        p = jnp.exp2(s - m_new).astype(BF16)
        vt_aug = jnp.concatenate([vt_ref[:, pl.ds(off, tq)], jnp.ones((ONES_ROWS, tq), BF16)], axis=0)
        acc_sc[...] = alpha * acc_sc[...] + jnp.dot(vt_aug, p, preferred_element_type=F32)
        m_sc[...] = m_new

    scores(0, sa_ref)

    def pair(jj, carry):
        j = 2 * jj
        scores(j + 1, sb_ref)
        softmax_pv(j, sa_ref, False)
        scores(j + 2, sa_ref)
        softmax_pv(j + 1, sb_ref, False)
        return carry

    lax.fori_loop(0, i // 2, pair, 0)

    @pl.when(i % 2 == 1)
    def _():
        scores(i, sb_ref)
        softmax_pv(i - 1, sa_ref, False)
        softmax_pv(i, sb_ref, True)

    @pl.when(i % 2 == 0)
    def _():
        softmax_pv(i, sa_ref, True)

    lv = lam_ref[...]
    lam = (jnp.exp(jnp.sum(lv[0:1] * lv[1:2], axis=-1, keepdims=True))
           - jnp.exp(jnp.sum(lv[2:3] * lv[3:4], axis=-1, keepdims=True)) + lam_init)
    acc = acc_sc[:DA_DV, :]
    l = acc_sc[DA_DV:DA_DV + 1, :]
    o = acc[:, :tq] / l[:, :tq] - lam * (acc[:, tq:] / l[:, tq:])
    y = o * lax.rsqrt(jnp.mean(o * o, axis=0, keepdims=True) + EPS) * sub_ref[...] * (1.0 - lam_init)
    o_ref[...] = y.T.astype(o_ref.dtype)


def _attention(z, qt, vt, lam_vecs, subln_col, bsz, seq, lam_init):
    t = z.shape[0]
    tq = min(512, seq)
    nq = seq // tq
    return pl.pallas_call(
        functools.partial(_attn_kernel, tq=tq, lam_init=lam_init),
        grid=(bsz, DA_HEADS, nq),
        in_specs=[pl.BlockSpec((DA_DV, tq), lambda b, h, i: (h, b * nq + i)),
                  pl.BlockSpec((seq, DA_DV), lambda b, h, i: (b, COL_K * DA_HEADS + h)),
                  pl.BlockSpec((DA_DV, seq), lambda b, h, i: (h, b)),
                  pl.BlockSpec((4, DA_DK), lambda b, h, i: (0, 0)),
                  pl.BlockSpec((DA_DV, 1), lambda b, h, i: (0, 0))],
        out_specs=pl.BlockSpec((tq, DA_DV), lambda b, h, i: (b * nq + i, h)),
        out_shape=jax.ShapeDtypeStruct((t, MIX_W), BF16),
        scratch_shapes=[pltpu.VMEM((tq, 2 * tq), F32), pltpu.VMEM((tq, 2 * tq), F32),
                        pltpu.VMEM((1, 2 * tq), F32), pltpu.VMEM((DA_DV + ONES_ROWS, 2 * tq), F32)],
        compiler_params=_params("arbitrary", "arbitrary", "arbitrary"),
        name="diff_attn",
    )(qt, z, vt, lam_vecs, subln_col)


def _gelu(x):
    return 0.5 * x * (1.0 + lax.erf(x * math.sqrt(0.5)))


def _gmlp_kernel(u_ref, v_ref, g_ref, b_ref, ws_ref, bs_ref, o_ref, *, bm):
    v = _gelu(v_ref[...].astype(F32))
    mu = jnp.mean(v, axis=-1, keepdims=True)
    vc = v - mu
    var = jnp.mean(vc * vc, axis=-1, keepdims=True)
    vn = (vc * lax.rsqrt(var + EPS) * g_ref[...] + b_ref[...]).astype(BF16)
    r = lax.broadcasted_iota(I32, (GM_CHUNK, GM_CHUNK), 0)
    c = lax.broadcasted_iota(I32, (GM_CHUNK, GM_CHUNK), 1)
    for g in range(GM_GROUPS):
        cols = slice(g * GM_CHUNK, (g + 1) * GM_CHUNK)
        wm = jnp.where(c <= r, ws_ref[g], 0.0).astype(BF16)
        bias = bs_ref[g]
        for n in range(bm // GM_CHUNK):
            rows = slice(n * GM_CHUNK, (n + 1) * GM_CHUNK)
            sv = jnp.dot(wm, vn[rows, cols], preferred_element_type=F32) + bias
            u = _gelu(u_ref[rows, cols].astype(F32))
            o_ref[rows, cols] = (u * sv).astype(o_ref.dtype)


def _gmlp(z, ln_g, ln_b, ws, bs_b):
    t = z.shape[0]
    bm = 512
    full = lambda shape: pl.BlockSpec(shape, lambda i: (0,) * len(shape))
    return pl.pallas_call(
        functools.partial(_gmlp_kernel, bm=bm),
        grid=(t // bm,),
        in_specs=[pl.BlockSpec((bm, MIX_W), lambda i: (i, COL_GU)),
                  pl.BlockSpec((bm, MIX_W), lambda i: (i, COL_GV)),
                  full((1, MIX_W)), full((1, MIX_W)),
                  full((GM_GROUPS, GM_CHUNK, GM_CHUNK)), full((GM_GROUPS, GM_CHUNK, GM_CHUNK))],
        out_specs=pl.BlockSpec((bm, MIX_W), lambda i: (i, 0)),
        out_shape=jax.ShapeDtypeStruct((t, MIX_W), BF16),
        compiler_params=_params("arbitrary"),
        name="gmlp",
    )(z, z, ln_g, ln_b, ws, bs_b)


def _pool_kernel(z_ref, halo_ref, pw_ref, ps_ref, o_ref, *, bm, tps):
    p0 = (pl.program_id(0) % tps) * bm
    d = (lax.broadcasted_iota(I32, (bm, bm), 0) - lax.broadcasted_iota(I32, (bm, bm), 1))
    dh = (lax.broadcasted_iota(I32, (bm, HALO), 0) + HALO - lax.broadcasted_iota(I32, (bm, HALO), 1))
    pos = p0 + lax.broadcasted_iota(I32, (bm, 1), 0)
    for g, w in enumerate(POOL_WINDOWS):
        cols = slice(g * POOL_GC, (g + 1) * POOL_GC)
        x = z_ref[:, cols]
        hx = halo_ref[:, cols]
        hx = jnp.where(p0 > 0, hx, jnp.zeros_like(hx))
        band = jnp.where(d >= 0, jnp.where(d < w, 1.0, 0.0), 0.0).astype(BF16)
        band_h = jnp.where(dh < w, 1.0, 0.0).astype(BF16)
        win = (jnp.dot(band, x, preferred_element_type=F32)
               + jnp.dot(band_h, hx, preferred_element_type=F32))
        cnt = jnp.minimum(pos + 1, w).astype(F32)
        pooled = (win / cnt - x.astype(F32)).astype(BF16)
        mixed = jnp.dot(pooled, pw_ref[g], preferred_element_type=F32)
        o_ref[:, cols] = (mixed * ps_ref[:, cols]).astype(o_ref.dtype)


def _halo_spec(bm, col):
    return pl.BlockSpec((HALO, MIX_W), lambda i: (jnp.maximum(i * (bm // HALO) - 1, 0), col))


def _pool(z, pw, ps, seq):
    t = z.shape[0]
    bm = min(512, seq)
    ng = len(POOL_WINDOWS)
    return pl.pallas_call(
        functools.partial(_pool_kernel, bm=bm, tps=seq // bm),
        grid=(t // bm,),
        in_specs=[pl.BlockSpec((bm, MIX_W), lambda i: (i, COL_POOL)),
                  _halo_spec(bm, COL_POOL),
                  pl.BlockSpec((ng, POOL_GC, POOL_GC), lambda i: (0, 0, 0)),
                  pl.BlockSpec((1, MIX_W), lambda i: (0, 0))],
        out_specs=pl.BlockSpec((bm, MIX_W), lambda i: (i, 0)),
        out_shape=jax.ShapeDtypeStruct((t, MIX_W), BF16),
        compiler_params=_params("arbitrary"),
        name="pool",
    )(z, z, pw, ps)


def _conv_kernel(b_ref, c_ref, h_ref, hc_ref, hh_ref, cw_ref, o_ref, *, tps):
    first = (pl.program_id(0) % tps) == 0
    xin = c_ref[...].astype(F32) * h_ref[...].astype(F32)
    hal = hc_ref[...].astype(F32) * hh_ref[...].astype(F32)
    hal = jnp.where(first, 0.0, hal)
    hm1 = hal[HALO - 1:HALO, :]
    hm2 = hal[HALO - 2:HALO - 1, :]
    row = lax.broadcasted_iota(I32, xin.shape, 0)
    s1 = jnp.where(row == 0, hm1, pltpu.roll(xin, 1, 0))
    s2 = jnp.where(row == 0, hm2, jnp.where(row == 1, hm1, pltpu.roll(xin, 2, 0)))
    cw = cw_ref[...]
    y = cw[0:1] * s2 + cw[1:2] * s1 + cw[2:3] * xin
    o_ref[...] = (b_ref[...].astype(F32) * y).astype(o_ref.dtype)


def _conv(z, cw, seq):
    t = z.shape[0]
    bm = min(512, seq)
    blk = lambda col: pl.BlockSpec((bm, MIX_W), lambda i: (i, col))
    return pl.pallas_call(
        functools.partial(_conv_kernel, tps=seq // bm),
        grid=(t // bm,),
        in_specs=[blk(COL_CB), blk(COL_CC), blk(COL_CH), _halo_spec(bm, COL_CC), _halo_spec(bm, COL_CH),
                  pl.BlockSpec(cw.shape, lambda i: (0, 0))],
        out_specs=pl.BlockSpec((bm, MIX_W), lambda i: (i, 0)),
        out_shape=jax.ShapeDtypeStruct((t, MIX_W), BF16),
        compiler_params=_params("arbitrary"),
        name="short_conv",
    )(z, z, z, z, z, cw)


def _merge_kernel(h_ref, ya_ref, yb_ref, yc_ref, yd_ref, wg_ref, bg_ref, wb_ref, o_ref):
    h = h_ref[...]
    bg = bg_ref[...]
    merged = None
    for i, y_ref in enumerate((ya_ref, yb_ref, yc_ref, yd_ref)):
        gate = jax.nn.sigmoid(jnp.dot(h, wg_ref[i], preferred_element_type=F32) + bg[i:i + 1])
        term = gate * jnp.dot(y_ref[...], wb_ref[i], preferred_element_type=F32)
        merged = term if merged is None else merged + term
    o_ref[...] = merged.astype(o_ref.dtype)


def _merge(h, ys, wg, bg, wb, layer):
    t, d = h.shape
    bm, bn = 1024, 256
    yspec = pl.BlockSpec((bm, MIX_W), lambda i, j: (i, 0))
    return pl.pallas_call(
        _merge_kernel,
        grid=(t // bm, d // bn),
        in_specs=[pl.BlockSpec((bm, d), lambda i, j: (i, 0)), yspec, yspec, yspec, yspec,
                  pl.BlockSpec((None, N_BRANCH, d, bn), lambda i, j: (layer, 0, 0, j)),
                  pl.BlockSpec((N_BRANCH, bn), lambda i, j: (0, j)),
                  pl.BlockSpec((None, N_BRANCH, MIX_W, bn), lambda i, j: (layer, 0, 0, j))],
        out_specs=pl.BlockSpec((bm, bn), lambda i, j: (i, j)),
        out_shape=jax.ShapeDtypeStruct((t, d), BF16),
        compiler_params=_params("arbitrary", "arbitrary"),
        name="branch_merge",
    )(h, *ys, wg, bg, wb)


def _outproj_kernel(m_ref, wo_ref, x_ref, g1_ref, ng_ref, sc_ref, sh_ref, rw_ref, rb_ref,
                    xo_ref, hp_ref, ti_ref, tw_ref, tr_ref, cnt_ref, carry_sc):
    out = jnp.dot(m_ref[...], wo_ref[...], preferred_element_type=F32)
    x_new = x_ref[...] + g1_ref[...] * out
    xo_ref[...] = x_new
    h2 = _rms_mod(x_new, ng_ref[...], sc_ref[...], sh_ref[...])
    half = h2.shape[1] // 2
    _store_row_tiles(hp_ref, _pack_pair(h2[:, :half], h2[:, half:]))

    logits = jnp.dot(h2, rw_ref[...], preferred_element_type=F32) + rb_ref[...]
    lane = lax.broadcasted_iota(I32, logits.shape, 1).astype(F32)
    vals, idxs = [], []
    cur = logits
    for _ in range(TOP_K):
        m = jnp.max(cur, axis=-1, keepdims=True)
        idx = jnp.min(jnp.where(cur == m, lane, float(LANES)), axis=-1, keepdims=True)
        vals.append(m)
        idxs.append(idx)
        cur = jnp.where(lane == idx, -3e38, cur)
    es = [jnp.exp(v - vals[0]) for v in vals]
    den = es[0] + es[1] + es[2] + es[3]

    @pl.when(pl.program_id(0) == 0)
    def _():
        carry_sc[...] = jnp.zeros(carry_sc.shape, F32)

    onehot = jnp.zeros(logits.shape, F32)
    for k in range(TOP_K):
        onehot = onehot + jnp.where(lane == idxs[k], 1.0, 0.0)
    bm = logits.shape[0]
    tri = jnp.where(lax.broadcasted_iota(I32, (bm, bm), 0) > lax.broadcasted_iota(I32, (bm, bm), 1),
                    1.0, 0.0).astype(BF16)
    before = jnp.dot(tri, onehot.astype(BF16), preferred_element_type=F32) + carry_sc[...]
    carry = carry_sc[...] + jnp.sum(onehot, axis=0, keepdims=True)
    carry_sc[...] = carry
    cnt_ref[...] = jnp.broadcast_to(carry, cnt_ref.shape)

    ti = jnp.zeros(logits.shape, F32)
    tw = jnp.zeros(logits.shape, F32)
    tr = jnp.zeros(logits.shape, F32)
    for k in range(TOP_K):
        rank = jnp.sum(jnp.where(lane == idxs[k], before, 0.0), axis=-1, keepdims=True)
        ti = jnp.where(lane == float(k), idxs[k], ti)
        tw = jnp.where(lane == float(k), es[k] / den, tw)
        tr = jnp.where(lane == float(k), rank, tr)
    ti_ref[...] = ti.astype(I32)
    tw_ref[...] = tw
    tr_ref[...] = tr.astype(I32)


def _outproj(merged, wo, layer, x, g1, ng, sc, sh, rw, rb, seq):
    t, d = x.shape
    bm = 512
    tps = seq // bm
    row = lambda w: pl.BlockSpec((bm, w), lambda i: (i, 0))
    vec = pl.BlockSpec((None, 1, d), lambda i: (i // tps, 0, 0))
    full = lambda a: pl.BlockSpec(a.shape, lambda i: (0, 0))
    return pl.pallas_call(
        _outproj_kernel,
        grid=(t // bm,),
        in_specs=[row(d), pl.BlockSpec((None, d, d), lambda i: (layer, 0, 0)), row(d), vec, full(ng), vec, vec,
                  full(rw), full(rb)],
        out_specs=[row(d), pl.BlockSpec((bm * SUBLANES, LANES), lambda i: (i, 0)),
                   row(LANES), row(LANES), row(LANES),
                   pl.BlockSpec((SUBLANES, LANES), lambda i: (0, 0))],
        out_shape=[jax.ShapeDtypeStruct((t, d), F32), jax.ShapeDtypeStruct((t * SUBLANES, LANES), U32),
                   jax.ShapeDtypeStruct((t, LANES), I32), jax.ShapeDtypeStruct((t, LANES), F32),
                   jax.ShapeDtypeStruct((t, LANES), I32), jax.ShapeDtypeStruct((SUBLANES, LANES), F32)],
        scratch_shapes=[pltpu.VMEM((1, LANES), F32)],
        compiler_params=_params("arbitrary"),
        name="outproj_router",
    )(merged, wo, x, g1, ng, sc, sh, rw, rb)


def _row_copy(src_ref, src_row, dst_ref, dst_row, sem):
    return pltpu.make_async_copy(src_ref.at[pl.ds(pl.multiple_of(src_row * SUBLANES, SUBLANES), SUBLANES), :],
                                 dst_ref.at[pl.ds(pl.multiple_of(dst_row * SUBLANES, SUBLANES), SUBLANES), :],
                                 sem)


def _dispatch_kernel(pad_lo_ref, pad_hi_ref, nu_ref, dest_ref, hp_ref, xb_hbm, zeros_sc, sem, zsem,
                     *, bm, n_blk):
    blk_rows = MOE_BLOCK * SUBLANES

    @pl.when(pl.program_id(0) == 0)
    def _():
        zeros_sc[...] = jnp.zeros(zeros_sc.shape, zeros_sc.dtype)

        def pad_rows(act):
            def per_expert(e, carry):
                def per_row(r, c2):
                    act(_row_copy(zeros_sc, 0, xb_hbm, r, zsem))
                    return c2
                return lax.fori_loop(pad_lo_ref[e], pad_hi_ref[e], per_row, carry)
            lax.fori_loop(0, N_EXPERTS, per_expert, 0)

        def tail_blocks(act):
            def per_block(b, carry):
                off = pl.multiple_of(b * blk_rows, blk_rows)
                act(pltpu.make_async_copy(zeros_sc, xb_hbm.at[pl.ds(off, blk_rows), :], zsem))
                return carry
            lax.fori_loop(nu_ref[0], n_blk, per_block, 0)

        pad_rows(lambda cp: cp.start())
        tail_blocks(lambda cp: cp.start())
        pad_rows(lambda cp: cp.wait())
        tail_blocks(lambda cp: cp.wait())

    def issue(r, carry):
        for k in range(TOP_K):
            _row_copy(hp_ref, r, xb_hbm, dest_ref[0, 0, r * TOP_K + k], sem).start(priority=k % 2)
        return carry

    lax.fori_loop(0, bm, issue, 0, unroll=4)
    for k in range(TOP_K):
        pltpu.make_async_copy(hp_ref, xb_hbm.at[pl.ds(0, bm * SUBLANES), :], sem).wait()


def _dispatch(pad_lo, pad_hi, n_used, dest, hp, n_rows):
    t = dest.shape[0]
    bm = 512
    nt = t // bm
    n_blk = n_rows // MOE_BLOCK
    return pl.pallas_call(
        functools.partial(_dispatch_kernel, bm=bm, n_blk=n_blk),
        grid_spec=pltpu.PrefetchScalarGridSpec(
            num_scalar_prefetch=3,
            grid=(nt,),
            in_specs=[pl.BlockSpec((1, 1, bm * TOP_K), lambda i, *_: (i, 0, 0), memory_space=pltpu.SMEM),
                      pl.BlockSpec((bm * SUBLANES, LANES), lambda i, *_: (i, 0))],
            out_specs=pl.BlockSpec(memory_space=pl.ANY),
            scratch_shapes=[pltpu.VMEM((MOE_BLOCK * SUBLANES, LANES), U32),
                            pltpu.SemaphoreType.DMA(()), pltpu.SemaphoreType.DMA(())]),
        out_shape=jax.ShapeDtypeStruct((n_rows * SUBLANES, LANES), U32),
        compiler_params=_params("arbitrary"),
        name="moe_dispatch",
    )(pad_lo, pad_hi, n_used, dest.reshape(nt, 1, bm * TOP_K), hp)


def _expert_kernel(be_ref, nu_ref, xb_ref, wgu_ref, bgu_ref, wdn_ref, bdn_ref, o_ref):
    b = pl.program_id(0)
    rows = MOE_BLOCK

    @pl.when(b < nu_ref[0])
    def _():
        los, his = [], []
        for c in range(SUBLANES):
            lo, hi = _unpack_pair(_load_row_tile_col(xb_ref, c, rows))
            los.append(lo.astype(BF16))
            his.append(hi.astype(BF16))
        lo = jnp.concatenate(los, axis=1)
        hi = jnp.concatenate(his, axis=1)
        half = lo.shape[1]
        gu = (jnp.dot(lo, wgu_ref[:half, :], preferred_element_type=F32)
              + jnp.dot(hi, wgu_ref[half:, :], preferred_element_type=F32) + bgu_ref[...])
        g = jnp.minimum(gu[:, :D_EXPERT], SWIGLU_LIMIT)
        u = jnp.clip(gu[:, D_EXPERT:], -SWIGLU_LIMIT, SWIGLU_LIMIT)
        act = (u + 1.0) * (g * jax.nn.sigmoid(SWIGLU_ALPHA * g))
        out = jnp.dot(act.astype(BF16), wdn_ref[...], preferred_element_type=F32) + bdn_ref[...]
        _store_row_tiles(o_ref, _pack_pair(out[:, :half], out[:, half:]))

    @pl.when(b >= nu_ref[0])
    def _():
        o_ref[...] = jnp.zeros(o_ref.shape, o_ref.dtype)


def _experts(block_e, n_used, xb, w_gu, b_gu, w_dn, b_dn, layer):
    n_blk = xb.shape[0] // (MOE_BLOCK * SUBLANES)
    _, ne, d, _ = w_gu.shape
    blk = pl.BlockSpec((MOE_BLOCK * SUBLANES, LANES), lambda b, be, nu: (b, 0))
    blk_in = pl.BlockSpec((MOE_BLOCK * SUBLANES, LANES), lambda b, be, nu: (jnp.minimum(b, nu[0] - 1), 0))
    return pl.pallas_call(
        _expert_kernel,
        grid_spec=pltpu.PrefetchScalarGridSpec(
            num_scalar_prefetch=2,
            grid=(n_blk,),
            in_specs=[blk_in,
                      pl.BlockSpec((None, None, d, 2 * D_EXPERT), lambda b, be, nu: (layer, be[b], 0, 0)),
                      pl.BlockSpec((None, 1, 2 * D_EXPERT), lambda b, be, nu: (be[b], 0, 0)),
                      pl.BlockSpec((None, None, D_EXPERT, d), lambda b, be, nu: (layer, be[b], 0, 0)),
                      pl.BlockSpec((None, 1, d), lambda b, be, nu: (be[b], 0, 0))],
            out_specs=blk),
        out_shape=jax.ShapeDtypeStruct(xb.shape, U32),
        compiler_params=_params("arbitrary"),
        name="moe_experts",
    )(block_e, n_used, xb, w_gu, b_gu.reshape(ne, 1, -1), w_dn, b_dn.reshape(ne, 1, -1))


def _combine_kernel(dest_ref, dnext_ref, tw_ref, x_ref, g2_ref, ng_ref, sc_ref, sh_ref, outp_hbm, *rest,
                    bm, final):
    if final:
        ho_ref, buf, sem = rest
    else:
        xo_ref, ho_ref, buf, sem = rest
    i = pl.program_id(0)
    slot = i % 2

    def gather_tile(d_ref, s):
        def issue(r, carry):
            for k in range(TOP_K):
                _row_copy(outp_hbm, d_ref[0, 0, r * TOP_K + k], buf.at[s, k], r, sem.at[s]).start(priority=k % 2)
            return carry

        lax.fori_loop(0, bm, issue, 0, unroll=4)

    @pl.when(i == 0)
    def _():
        gather_tile(dest_ref, 0)

    @pl.when(i + 1 < pl.num_programs(0))
    def _():
        gather_tile(dnext_ref, 1 - slot)

    for k in range(TOP_K):
        pltpu.make_async_copy(outp_hbm.at[pl.ds(0, bm * SUBLANES), :], buf.at[slot, k], sem.at[slot]).wait()

    tw = tw_ref[...]
    wk = [tw[:, k:k + 1] for k in range(TOP_K)]
    y_lo, y_hi = [], []
    for c in range(SUBLANES):
        a_lo = a_hi = None
        for k in range(TOP_K):
            lo, hi = _unpack_pair(buf[slot, k, pl.ds(c, bm, stride=SUBLANES), :])
            a_lo = wk[k] * lo if a_lo is None else a_lo + wk[k] * lo
            a_hi = wk[k] * hi if a_hi is None else a_hi + wk[k] * hi
        y_lo.append(a_lo)
        y_hi.append(a_hi)
    x_new = x_ref[...] + g2_ref[...] * jnp.concatenate(y_lo + y_hi, axis=1)
    if final:
        ho_ref[...] = x_new * lax.rsqrt(jnp.mean(x_new * x_new, axis=-1, keepdims=True) + EPS) * ng_ref[...]
    else:
        xo_ref[...] = x_new
        ho_ref[...] = _rms_mod(x_new, ng_ref[...], sc_ref[...], sh_ref[...]).astype(BF16)


def _combine(dest, tw, x, g2, ng, sc, sh, outp, seq, final):
    t, d = x.shape
    bm = 256
    tps = seq // bm
    nt = t // bm
    row = lambda w: pl.BlockSpec((bm, w), lambda i: (i, 0))
    vec = pl.BlockSpec((None, 1, d), lambda i: (i // tps, 0, 0))
    dest3 = dest.reshape(nt, 1, bm * TOP_K)
    if final:
        out_specs = [row(d)]
        out_shape = [jax.ShapeDtypeStruct((t, d), F32)]
    else:
        out_specs = [row(d), row(d)]
        out_shape = [jax.ShapeDtypeStruct((t, d), F32), jax.ShapeDtypeStruct((t, d), BF16)]
    return pl.pallas_call(
        functools.partial(_combine_kernel, bm=bm, final=final),
        grid=(nt,),
        in_specs=[pl.BlockSpec((1, 1, bm * TOP_K), lambda i: (i, 0, 0), memory_space=pltpu.SMEM),
                  pl.BlockSpec((1, 1, bm * TOP_K), lambda i: (jnp.minimum(i + 1, nt - 1), 0, 0),
                               memory_space=pltpu.SMEM),
                  row(LANES), row(d), vec, pl.BlockSpec((1, d), lambda i: (0, 0)), vec, vec,
                  pl.BlockSpec(memory_space=pl.ANY)],
        out_specs=out_specs,
        out_shape=out_shape,
        scratch_shapes=[pltpu.VMEM((2, TOP_K, bm * SUBLANES, LANES), U32), pltpu.SemaphoreType.DMA((2,))],
        compiler_params=_params("arbitrary"),
        name="moe_combine",
    )(dest3, dest3, tw, x, g2, ng, sc, sh, outp)


def _routing_tables(top_i, rank, counts_f):
    t = top_i.shape[0]
    n_asg = t * TOP_K
    counts = counts_f.astype(I32)
    padded = ((counts + MOE_BLOCK - 1) // MOE_BLOCK) * MOE_BLOCK
    pend = jnp.cumsum(padded)
    pstart = pend - padded
    experts = jnp.arange(N_EXPERTS, dtype=I32)
    start_of = jnp.sum(jnp.where(top_i[:, :, None] == experts, pstart, 0), axis=-1)
    dest = (start_of + rank).astype(I32)
    n_rows = ((n_asg + N_EXPERTS * MOE_BLOCK + MOE_BLOCK - 1) // MOE_BLOCK) * MOE_BLOCK
    n_blk = n_rows // MOE_BLOCK
    first_row = jnp.arange(n_blk, dtype=I32) * MOE_BLOCK
    block_e = jnp.minimum(jnp.sum((pend[None, :] <= first_row[:, None]).astype(I32), axis=1), N_EXPERTS - 1)
    n_used = (pend[-1:] // MOE_BLOCK).astype(I32)
    return dest, block_e, n_used, n_rows, (pstart + counts).astype(I32), pend.astype(I32)


def _rotary_tables(positions):
    inv_freq = ROPE_THETA ** (-jnp.arange(0, ROT_DIM, 2, dtype=F32) / ROT_DIM)
    ang = positions.astype(F32).reshape(-1, 1) * inv_freq
    cos, sin = jnp.cos(ang), jnp.sin(ang)
    t = ang.shape[0]
    pad = jnp.zeros((t, DA_DK - ROT_DIM), F32)
    c64 = jnp.concatenate([cos, cos, pad + 1.0], axis=1)
    s1_64 = jnp.concatenate([jnp.zeros_like(sin), sin, pad], axis=1)
    s2_64 = jnp.concatenate([-sin, jnp.zeros_like(sin), pad], axis=1)
    rep = LANES // DA_DK
    return jnp.tile(c64, (1, rep)), jnp.tile(s1_64, (1, rep)), jnp.tile(s2_64, (1, rep))


def kernel(x, c, positions, w_ada, b_ada, norm_mix, norm_ffn, w_in, diff_lambda, diff_subln, gmlp_ln_g, gmlp_ln_b, gmlp_w_spatial, gmlp_b_spatial, pool_w, pool_scale, conv_w, w_branch, w_gate, b_gate, w_out, router_w, router_b, expert_w_gu, expert_b_gu, expert_w_down, expert_b_down, final_norm):
    bsz, seq, d = x.shape
    depth = w_ada.shape[0]
    t = bsz * seq
    assert seq % 512 == 0 and d == 2 * MIX_W and d // 2 == SUBLANES * LANES

    rot_c, rot_s1, rot_s2 = _rotary_tables(positions)
    c_pad = jnp.zeros((SUBLANES, d), F32).at[:bsz].set(c)
    mod = _adaln(c_pad, w_ada, b_ada)[:, :bsz].reshape(depth, bsz, 6, 1, d)
    rw_pad = jnp.zeros((depth, d, LANES), F32).at[:, :, :N_EXPERTS].set(router_w)
    rb_pad = jnp.full((depth, 1, LANES), NEG_INF, F32).at[:, 0, :N_EXPERTS].set(router_b)

    w_in_b, w_gate_b, w_branch_b, w_out_b = (w.astype(BF16) for w in (w_in, w_gate, w_branch, w_out))
    w_gu_b, w_dn_b = expert_w_gu.astype(BF16), expert_w_down.astype(BF16)

    xf = x.reshape(t, d)
    h = None
    for l in range(depth):
        lam_init = 0.8 - 0.6 * math.exp(-0.3 * l)
        sh1, sc1, g1, sh2, sc2, g2 = (mod[l, :, i] for i in range(6))
        if l == 0:
            h = _norm_mod(xf, norm_mix[l].reshape(1, d), sc1, sh1, seq)
        z, qt, vt = _inproj(h, w_in_b, l, rot_c, rot_s1, rot_s2)
        y_a = _attention(z, qt, vt, diff_lambda[l], diff_subln[l].reshape(DA_DV, 1), bsz, seq, lam_init)
        bs_b = jnp.broadcast_to(gmlp_b_spatial[l][:, :, None], (GM_GROUPS, GM_CHUNK, GM_CHUNK))
        y_b = _gmlp(z, gmlp_ln_g[l].reshape(1, MIX_W), gmlp_ln_b[l].reshape(1, MIX_W),
                    gmlp_w_spatial[l], bs_b)
        y_c = _pool(z, pool_w[l].astype(BF16), pool_scale[l].reshape(1, MIX_W), seq)
        y_d = _conv(z, conv_w[l], seq)
        merged = _merge(h, (y_a, y_b, y_c, y_d), w_gate_b, b_gate[l], w_branch_b, l)
        xf, hp, ti, tw, tr, cnt = _outproj(merged, w_out_b, l, xf, g1, norm_ffn[l].reshape(1, d),
                                           sc2, sh2, rw_pad[l], rb_pad[l], seq)
        dest, block_e, n_used, n_rows, pad_lo, pad_hi = _routing_tables(ti[:, :TOP_K], tr[:, :TOP_K],
                                                                        cnt[0, :N_EXPERTS])
        xb = _dispatch(pad_lo, pad_hi, n_used, dest, hp, n_rows)
        outp = _experts(block_e, n_used, xb, w_gu_b, expert_b_gu[l], w_dn_b, expert_b_down[l], l)
        if l + 1 < depth:
            nsh1, nsc1 = mod[l + 1, :, 0], mod[l + 1, :, 1]
            xf, h = _combine(dest, tw, xf, g2, norm_mix[l + 1].reshape(1, d), nsc1, nsh1, outp, seq, False)
        else:
            zero = jnp.zeros((bsz, 1, d), F32)
            (out,) = _combine(dest, tw, xf, g2, final_norm.reshape(1, d), zero, zero, outp, seq, True)
    return out.reshape(bsz, seq, d)
```

```python
import functools
import math

import jax
import jax.numpy as jnp
from jax import lax
from jax.experimental import pallas as pl
from jax.experimental.pallas import tpu as pltpu

F32 = jnp.float32
BF16 = jnp.bfloat16
U32 = jnp.uint32
I32 = jnp.int32

MIX_W = 1024
DA_HEADS = 8
DA_DK = 64
DA_DV = 128
ROT_DIM = 16
ROPE_THETA = 500000.0
GM_GROUPS = 8
GM_CHUNK = 128
POOL_WINDOWS = (2, 4, 8, 16)
POOL_GC = 256
N_BRANCH = 4
N_EXPERTS = 32
TOP_K = 4
D_EXPERT = 1024
SWIGLU_LIMIT = 7.0
SWIGLU_ALPHA = 1.702
MOE_BLOCK = 512
EPS = 1e-6
NEG_INF = -1e30

LANES = 128
SUBLANES = 8
HALO = 16
ONES_ROWS = 16
VMEM_LIMIT = 56 * 1024 * 1024

COL_K, COL_GU, COL_GV, COL_POOL, COL_CB, COL_CC, COL_CH = range(7)
IN_Q, IN_K, IN_V = 0, 1, 2


def _params(*sem):
    return pltpu.CompilerParams(dimension_semantics=sem, vmem_limit_bytes=VMEM_LIMIT)


def _rms_mod(x, g, sc, sh):
    y = x * lax.rsqrt(jnp.mean(x * x, axis=-1, keepdims=True) + EPS) * g
    return y * (1.0 + sc) + sh


def _pack_pair(lo, hi):
    lo_bits = lax.bitcast_convert_type(lo.astype(BF16).astype(F32), U32) >> 16
    hi_bits = lax.bitcast_convert_type(hi.astype(BF16).astype(F32), U32) & jnp.uint32(0xFFFF0000)
    return hi_bits | lo_bits


def _unpack_pair(w):
    lo = lax.bitcast_convert_type(w << 16, F32)
    hi = lax.bitcast_convert_type(w & jnp.uint32(0xFFFF0000), F32)
    return lo, hi


def _store_row_tiles(ref, packed):
    n = packed.shape[0]
    for c in range(SUBLANES):
        ref[pl.ds(c, n, stride=SUBLANES), :] = packed[:, c * LANES:(c + 1) * LANES]


def _load_row_tile_col(ref, c, n):
    return ref[pl.ds(c, n, stride=SUBLANES), :]


def _adaln_kernel(c_ref, w_ref, b_ref, o_ref):
    c = c_ref[...]
    o_ref[...] = jnp.dot(c * jax.nn.sigmoid(c), w_ref[...], preferred_element_type=F32) + b_ref[...]


def _adaln(c_pad, w_ada, b_ada):
    depth, d, n = w_ada.shape
    bn = 1536
    rows = c_pad.shape[0]
    return pl.pallas_call(
        _adaln_kernel,
        grid=(depth, n // bn),
        in_specs=[pl.BlockSpec((rows, d), lambda l, j: (0, 0)),
                  pl.BlockSpec((None, d, bn), lambda l, j: (l, 0, j)),
                  pl.BlockSpec((None, 1, bn), lambda l, j: (l, 0, j))],
        out_specs=pl.BlockSpec((None, rows, bn), lambda l, j: (l, 0, j)),
        out_shape=jax.ShapeDtypeStruct((depth, rows, n), F32),
        compiler_params=_params("arbitrary", "arbitrary"),
        name="adaln",
    )(c_pad, w_ada, b_ada.reshape(depth, 1, n))


def _norm_kernel(x_ref, g_ref, sc_ref, sh_ref, h_ref):
    h_ref[...] = _rms_mod(x_ref[...], g_ref[...], sc_ref[...], sh_ref[...]).astype(BF16)


def _norm_mod(x, g, sc, sh, seq):
    t, d = x.shape
    bm = 512
    tps = seq // bm
    vec = pl.BlockSpec((None, 1, d), lambda i: (i // tps, 0, 0))
    return pl.pallas_call(
        _norm_kernel,
        grid=(t // bm,),
        in_specs=[pl.BlockSpec((bm, d), lambda i: (i, 0)),
                  pl.BlockSpec((1, d), lambda i: (0, 0)), vec, vec],
        out_specs=pl.BlockSpec((bm, d), lambda i: (i, 0)),
        out_shape=jax.ShapeDtypeStruct((t, d), BF16),
        compiler_params=_params("arbitrary"),
        name="norm_mod",
    )(x, g, sc, sh)


def _inproj_kernel(h_ref, w_ref, c_ref, s1_ref, s2_ref, ct_ref, st_ref, z_ref, qt_ref, vt_ref):
    j = pl.program_id(1)
    acc = jnp.dot(h_ref[...], w_ref[...], preferred_element_type=F32)
    n_tiles = acc.shape[1] // LANES
    half = ROT_DIM // 2

    @pl.when(j == IN_Q)
    def _():
        scale = DA_DK ** -0.5 * math.log2(math.e)
        cos = ct_ref[...] * scale
        sin = st_ref[...] * scale
        for t in range(n_tiles):
            xt = acc[:, t * LANES:(t + 1) * LANES].T
            parts = []
            for comp in range(LANES // DA_DK):
                base = comp * DA_DK
                x1 = xt[base:base + half]
                x2 = xt[base + half:base + ROT_DIM]
                parts += [x1 * cos - x2 * sin, x2 * cos + x1 * sin, xt[base + ROT_DIM:base + DA_DK] * scale]
            qt_ref[t * LANES:(t + 1) * LANES, :] = jnp.concatenate(parts, axis=0).astype(BF16)

    @pl.when(j == IN_K)
    def _():
        for t in range(n_tiles):
            xt = acc[:, t * LANES:(t + 1) * LANES]
            r = (xt * c_ref[...] + pltpu.roll(xt, half, 1) * s1_ref[...]
                 + pltpu.roll(xt, LANES - half, 1) * s2_ref[...])
            z_ref[:, t * LANES:(t + 1) * LANES] = r.astype(BF16)

    @pl.when(j == IN_V)
    def _():
        for t in range(n_tiles):
            vt_ref[t * LANES:(t + 1) * LANES, :] = acc[:, t * LANES:(t + 1) * LANES].T.astype(BF16)

    @pl.when(j > IN_V)
    def _():
        z_ref[...] = acc.astype(BF16)


def _inproj(h, w_in, layer, rot_c, rot_s1, rot_s2, cos_t, sin_t):
    t, d = h.shape
    n = w_in.shape[2]
    bm, bn = 1024, MIX_W
    nz = n // bn - 2
    rot = pl.BlockSpec((bm, LANES), lambda i, j: (i, 0))
    rot_t = pl.BlockSpec((ROT_DIM // 2, bm), lambda i, j: (0, i))
    return pl.pallas_call(
        _inproj_kernel,
        grid=(t // bm, n // bn),
        in_specs=[pl.BlockSpec((bm, d), lambda i, j: (i, 0)),
                  pl.BlockSpec((None, d, bn), lambda i, j: (layer, 0, j)), rot, rot, rot, rot_t, rot_t],
        out_specs=[pl.BlockSpec((bm, bn), lambda i, j: (i, jnp.maximum(j - 2, 0))),
                   pl.BlockSpec((bn, bm), lambda i, j: (0, i)),
                   pl.BlockSpec((bn, bm), lambda i, j: (0, i))],
        out_shape=[jax.ShapeDtypeStruct((t, nz * bn), BF16),
                   jax.ShapeDtypeStruct((bn, t), BF16),
                   jax.ShapeDtypeStruct((bn, t), BF16)],
        compiler_params=_params("arbitrary", "arbitrary"),
        name="inproj",
    )(h, w_in, rot_c, rot_s1, rot_s2, cos_t, sin_t)


def _attn_kernel(qt_ref, k_ref, vt_ref, lam_ref, sub_ref, wgu_ref, wdn_ref, o_ref, wgu_o_ref, wdn_o_ref,
                 sa_ref, sb_ref, ma_sc, mb_sc, m_sc, acc_sc, *, tq, lam_init):
    i = pl.program_id(2)
    wgu_o_ref[...] = wgu_ref[...].astype(BF16)
    wdn_o_ref[...] = wdn_ref[...].astype(BF16)
    qt = qt_ref[...]
    row = lax.broadcasted_iota(I32, qt.shape, 0)
    zero = jnp.zeros_like(qt)
    qs = jnp.concatenate([jnp.where(row < DA_DK, qt, zero), jnp.where(row >= DA_DK, qt, zero)], axis=1)
    m_sc[...] = jnp.full(m_sc.shape, NEG_INF, F32)
    acc_sc[...] = jnp.zeros(acc_sc.shape, F32)

    def scores(j, s_ref, smax_ref):
        off = pl.multiple_of(j * tq, tq)
        s = jnp.dot(k_ref[pl.ds(off, tq), :], qs, preferred_element_type=F32)
        s_ref[...] = s
        smax_ref[...] = jnp.max(s, axis=0, keepdims=True)

    def softmax_pv(j, s_ref, smax_ref, masked):
        off = pl.multiple_of(j * tq, tq)
        s = s_ref[...]
        if masked:
            key = lax.broadcasted_iota(I32, s.shape, 0)
            c2 = lax.broadcasted_iota(I32, s.shape, 1)
            s = jnp.where(key <= jnp.where(c2 >= tq, c2 - tq, c2), s, NEG_INF)
            s_max = jnp.max(s, axis=0, keepdims=True)
        else:
            s_max = smax_ref[...]
        m_prev = m_sc[...]
        m_new = jnp.maximum(m_prev, s_max)
        alpha = jnp.exp2(m_prev - m_new)
        p = jnp.exp2(s - m_new).astype(BF16)
        vt_aug = jnp.concatenate([vt_ref[:, pl.ds(off, tq)], jnp.ones((ONES_ROWS, tq), BF16)], axis=0)
        acc_sc[...] = alpha * acc_sc[...] + jnp.dot(vt_aug, p, preferred_element_type=F32)
        m_sc[...] = m_new

    scores(0, sa_ref, ma_sc)

    def pair(jj, carry):
        j = 2 * jj
        scores(j + 1, sb_ref, mb_sc)
        softmax_pv(j, sa_ref, ma_sc, False)
        scores(j + 2, sa_ref, ma_sc)
        softmax_pv(j + 1, sb_ref, mb_sc, False)
        return carry

    lax.fori_loop(0, i // 2, pair, 0)

    @pl.when(i % 2 == 1)
    def _():
        scores(i, sb_ref, mb_sc)
        softmax_pv(i - 1, sa_ref, ma_sc, False)
        softmax_pv(i, sb_ref, mb_sc, True)

    @pl.when(i % 2 == 0)
    def _():
        softmax_pv(i, sa_ref, ma_sc, True)

    lv = lam_ref[...]
    lam = (jnp.exp(jnp.sum(lv[0:1] * lv[1:2], axis=-1, keepdims=True))
           - jnp.exp(jnp.sum(lv[2:3] * lv[3:4], axis=-1, keepdims=True)) + lam_init)
    acc = acc_sc[:DA_DV, :]
    l = acc_sc[DA_DV:DA_DV + 1, :]
    o = acc[:, :tq] / l[:, :tq] - lam * (acc[:, tq:] / l[:, tq:])
    y = o * lax.rsqrt(jnp.mean(o * o, axis=0, keepdims=True) + EPS) * sub_ref[...] * (1.0 - lam_init)
    o_ref[...] = y.T.astype(o_ref.dtype)


def _attention(z, qt, vt, lam_vecs, subln_col, w_gu, w_dn, layer, bsz, seq, lam_init):
    t = z.shape[0]
    tq = min(512, seq)
    nq = seq // tq
    steps = bsz * DA_HEADS * nq
    _, ne, d, gu_w = w_gu.shape
    gu_rows, dn_rows = ne * d // steps, ne * D_EXPERT // steps
    assert gu_rows * steps == ne * d and dn_rows * steps == ne * D_EXPERT and dn_rows % HALO == 0
    step = lambda b, h, i: (b * DA_HEADS + h) * nq + i
    y_a, gu_b, dn_b = pl.pallas_call(
        functools.partial(_attn_kernel, tq=tq, lam_init=lam_init),
        grid=(bsz, DA_HEADS, nq),
        in_specs=[pl.BlockSpec((DA_DV, tq), lambda b, h, i: (h, b * nq + i)),
                  pl.BlockSpec((seq, DA_DV), lambda b, h, i: (b, COL_K * DA_HEADS + h)),
                  pl.BlockSpec((DA_DV, seq), lambda b, h, i: (h, b)),
                  pl.BlockSpec((4, DA_DK), lambda b, h, i: (0, 0)),
                  pl.BlockSpec((DA_DV, 1), lambda b, h, i: (0, 0)),
                  pl.BlockSpec((None, gu_rows, gu_w), lambda b, h, i: (layer, step(b, h, i), 0)),
                  pl.BlockSpec((None, dn_rows, d), lambda b, h, i: (layer, step(b, h, i), 0))],
        out_specs=[pl.BlockSpec((tq, DA_DV), lambda b, h, i: (b * nq + i, h)),
                   pl.BlockSpec((gu_rows, gu_w), lambda b, h, i: (step(b, h, i), 0)),
                   pl.BlockSpec((dn_rows, d), lambda b, h, i: (step(b, h, i), 0))],
        out_shape=[jax.ShapeDtypeStruct((t, MIX_W), BF16),
                   jax.ShapeDtypeStruct((ne * d, gu_w), BF16),
                   jax.ShapeDtypeStruct((ne * D_EXPERT, d), BF16)],
        scratch_shapes=[pltpu.VMEM((tq, 2 * tq), F32), pltpu.VMEM((tq, 2 * tq), F32),
                        pltpu.VMEM((1, 2 * tq), F32), pltpu.VMEM((1, 2 * tq), F32),
                        pltpu.VMEM((1, 2 * tq), F32), pltpu.VMEM((DA_DV + ONES_ROWS, 2 * tq), F32)],
        compiler_params=_params("arbitrary", "arbitrary", "arbitrary"),
        name="diff_attn",
    )(qt, z, vt, lam_vecs, subln_col, w_gu.reshape(w_gu.shape[0], ne * d, gu_w),
      w_dn.reshape(w_dn.shape[0], ne * D_EXPERT, d))
    return y_a, gu_b.reshape(ne, d, gu_w), dn_b.reshape(ne, D_EXPERT, d)


def _gelu(x):
    return 0.5 * x * (1.0 + lax.erf(x * math.sqrt(0.5)))


def _gmlp_kernel(u_ref, v_ref, g_ref, b_ref, ws_ref, bs_ref, o_ref, *, bm):
    v = _gelu(v_ref[...].astype(F32))
    mu = jnp.mean(v, axis=-1, keepdims=True)
    vc = v - mu
    var = jnp.mean(vc * vc, axis=-1, keepdims=True)
    vn = (vc * lax.rsqrt(var + EPS) * g_ref[...] + b_ref[...]).astype(BF16)
    r = lax.broadcasted_iota(I32, (GM_CHUNK, GM_CHUNK), 0)
    c = lax.broadcasted_iota(I32, (GM_CHUNK, GM_CHUNK), 1)
    for g in range(GM_GROUPS):
        cols = slice(g * GM_CHUNK, (g + 1) * GM_CHUNK)
        wm = jnp.where(c <= r, ws_ref[g], 0.0).astype(BF16)
        bias = bs_ref[g]
        for n in range(bm // GM_CHUNK):
            rows = slice(n * GM_CHUNK, (n + 1) * GM_CHUNK)
            sv = jnp.dot(wm, vn[rows, cols], preferred_element_type=F32) + bias
            u = _gelu(u_ref[rows, cols].astype(F32))
            o_ref[rows, cols] = (u * sv).astype(o_ref.dtype)


def _gmlp(z, ln_g, ln_b, ws, bs_b):
    t = z.shape[0]
    bm = 512
    full = lambda shape: pl.BlockSpec(shape, lambda i: (0,) * len(shape))
    return pl.pallas_call(
        functools.partial(_gmlp_kernel, bm=bm),
        grid=(t // bm,),
        in_specs=[pl.BlockSpec((bm, MIX_W), lambda i: (i, COL_GU)),
                  pl.BlockSpec((bm, MIX_W), lambda i: (i, COL_GV)),
                  full((1, MIX_W)), full((1, MIX_W)),
                  full((GM_GROUPS, GM_CHUNK, GM_CHUNK)), full((GM_GROUPS, GM_CHUNK, GM_CHUNK))],
        out_specs=pl.BlockSpec((bm, MIX_W), lambda i: (i, 0)),
        out_shape=jax.ShapeDtypeStruct((t, MIX_W), BF16),
        compiler_params=_params("arbitrary"),
        name="gmlp",
    )(z, z, ln_g, ln_b, ws, bs_b)


def _pool_kernel(z_ref, halo_ref, pw_ref, ps_ref, o_ref, *, bm, tps):
    p0 = (pl.program_id(0) % tps) * bm
    d = (lax.broadcasted_iota(I32, (bm, bm), 0) - lax.broadcasted_iota(I32, (bm, bm), 1))
    dh = (lax.broadcasted_iota(I32, (bm, HALO), 0) + HALO - lax.broadcasted_iota(I32, (bm, HALO), 1))
    pos = p0 + lax.broadcasted_iota(I32, (bm, 1), 0)
    for g, w in enumerate(POOL_WINDOWS):
        cols = slice(g * POOL_GC, (g + 1) * POOL_GC)
        x = z_ref[:, cols]
        hx = halo_ref[:, cols]
        hx = jnp.where(p0 > 0, hx, jnp.zeros_like(hx))
        band = jnp.where(d >= 0, jnp.where(d < w, 1.0, 0.0), 0.0).astype(BF16)
        band_h = jnp.where(dh < w, 1.0, 0.0).astype(BF16)
        win = (jnp.dot(band, x, preferred_element_type=F32)
               + jnp.dot(band_h, hx, preferred_element_type=F32))
        cnt = jnp.minimum(pos + 1, w).astype(F32)
        pooled = (win / cnt - x.astype(F32)).astype(BF16)
        mixed = jnp.dot(pooled, pw_ref[g], preferred_element_type=F32)
        o_ref[:, cols] = (mixed * ps_ref[:, cols]).astype(o_ref.dtype)


def _halo_spec(bm, col):
    return pl.BlockSpec((HALO, MIX_W), lambda i: (jnp.maximum(i * (bm // HALO) - 1, 0), col))


def _pool(z, pw, ps, seq):
    t = z.shape[0]
    bm = min(512, seq)
    ng = len(POOL_WINDOWS)
    return pl.pallas_call(
        functools.partial(_pool_kernel, bm=bm, tps=seq // bm),
        grid=(t // bm,),
        in_specs=[pl.BlockSpec((bm, MIX_W), lambda i: (i, COL_POOL)),
                  _halo_spec(bm, COL_POOL),
                  pl.BlockSpec((ng, POOL_GC, POOL_GC), lambda i: (0, 0, 0)),
                  pl.BlockSpec((1, MIX_W), lambda i: (0, 0))],
        out_specs=pl.BlockSpec((bm, MIX_W), lambda i: (i, 0)),
        out_shape=jax.ShapeDtypeStruct((t, MIX_W), BF16),
        compiler_params=_params("arbitrary"),
        name="pool",
    )(z, z, pw, ps)


def _conv_kernel(b_ref, c_ref, h_ref, hc_ref, hh_ref, cw_ref, o_ref, *, tps):
    first = (pl.program_id(0) % tps) == 0
    xin = c_ref[...].astype(F32) * h_ref[...].astype(F32)
    hal = hc_ref[...].astype(F32) * hh_ref[...].astype(F32)
    hal = jnp.where(first, 0.0, hal)
    hm1 = hal[HALO - 1:HALO, :]
    hm2 = hal[HALO - 2:HALO - 1, :]
    row = lax.broadcasted_iota(I32, xin.shape, 0)
    s1 = jnp.where(row == 0, hm1, pltpu.roll(xin, 1, 0))
    s2 = jnp.where(row == 0, hm2, jnp.where(row == 1, hm1, pltpu.roll(xin, 2, 0)))
    cw = cw_ref[...]
    y = cw[0:1] * s2 + cw[1:2] * s1 + cw[2:3] * xin
    o_ref[...] = (b_ref[...].astype(F32) * y).astype(o_ref.dtype)


def _conv(z, cw, seq):
    t = z.shape[0]
    bm = min(512, seq)
    blk = lambda col: pl.BlockSpec((bm, MIX_W), lambda i: (i, col))
    return pl.pallas_call(
        functools.partial(_conv_kernel, tps=seq // bm),
        grid=(t // bm,),
        in_specs=[blk(COL_CB), blk(COL_CC), blk(COL_CH), _halo_spec(bm, COL_CC), _halo_spec(bm, COL_CH),
                  pl.BlockSpec(cw.shape, lambda i: (0, 0))],
        out_specs=pl.BlockSpec((bm, MIX_W), lambda i: (i, 0)),
        out_shape=jax.ShapeDtypeStruct((t, MIX_W), BF16),
        compiler_params=_params("arbitrary"),
        name="short_conv",
    )(z, z, z, z, z, cw)


def _merge_kernel(h_ref, ya_ref, yb_ref, yc_ref, yd_ref, wg_ref, bg_ref, wb_ref, o_ref):
    h = h_ref[...]
    bg = bg_ref[...]
    merged = None
    for i, y_ref in enumerate((ya_ref, yb_ref, yc_ref, yd_ref)):
        gate = jax.nn.sigmoid(jnp.dot(h, wg_ref[i], preferred_element_type=F32) + bg[i:i + 1])
        term = gate * jnp.dot(y_ref[...], wb_ref[i], preferred_element_type=F32)
        merged = term if merged is None else merged + term
    o_ref[...] = merged.astype(o_ref.dtype)


def _merge(h, ys, wg, bg, wb, layer):
    t, d = h.shape
    bm, bn = 1024, 256
    yspec = pl.BlockSpec((bm, MIX_W), lambda i, j: (i, 0))
    return pl.pallas_call(
        _merge_kernel,
        grid=(t // bm, d // bn),
        in_specs=[pl.BlockSpec((bm, d), lambda i, j: (i, 0)), yspec, yspec, yspec, yspec,
                  pl.BlockSpec((None, N_BRANCH, d, bn), lambda i, j: (layer, 0, 0, j)),
                  pl.BlockSpec((N_BRANCH, bn), lambda i, j: (0, j)),
                  pl.BlockSpec((None, N_BRANCH, MIX_W, bn), lambda i, j: (layer, 0, 0, j))],
        out_specs=pl.BlockSpec((bm, bn), lambda i, j: (i, j)),
        out_shape=jax.ShapeDtypeStruct((t, d), BF16),
        compiler_params=_params("arbitrary", "arbitrary"),
        name="branch_merge",
    )(h, *ys, wg, bg, wb)


def _outproj_kernel(m_ref, wo_ref, x_ref, g1_ref, ng_ref, sc_ref, sh_ref, rw_ref, rb_ref,
                    xo_ref, hp_ref, ti_ref, tw_ref, tr_ref, cnt_ref, carry_sc):
    out = jnp.dot(m_ref[...], wo_ref[...], preferred_element_type=F32)
    x_new = x_ref[...] + g1_ref[...] * out
    xo_ref[...] = x_new
    h2 = _rms_mod(x_new, ng_ref[...], sc_ref[...], sh_ref[...])
    half = h2.shape[1] // 2
    _store_row_tiles(hp_ref, _pack_pair(h2[:, :half], h2[:, half:]))

    logits = jnp.dot(h2, rw_ref[...], preferred_element_type=F32) + rb_ref[...]
    lane = lax.broadcasted_iota(I32, logits.shape, 1).astype(F32)
    vals, idxs = [], []
    cur = logits
    for _ in range(TOP_K):
        m = jnp.max(cur, axis=-1, keepdims=True)
        idx = jnp.min(jnp.where(cur == m, lane, float(LANES)), axis=-1, keepdims=True)
        vals.append(m)
        idxs.append(idx)
        cur = jnp.where(lane == idx, -3e38, cur)
    es = [jnp.exp(v - vals[0]) for v in vals]
    den = es[0] + es[1] + es[2] + es[3]

    @pl.when(pl.program_id(0) == 0)
    def _():
        carry_sc[...] = jnp.zeros(carry_sc.shape, F32)

    onehot = jnp.zeros(logits.shape, F32)
    for k in range(TOP_K):
        onehot = onehot + jnp.where(lane == idxs[k], 1.0, 0.0)
    bm = logits.shape[0]
    tri = jnp.where(lax.broadcasted_iota(I32, (bm, bm), 0) > lax.broadcasted_iota(I32, (bm, bm), 1),
                    1.0, 0.0).astype(BF16)
    before = jnp.dot(tri, onehot.astype(BF16), preferred_element_type=F32) + carry_sc[...]
    carry = carry_sc[...] + jnp.sum(onehot, axis=0, keepdims=True)
    carry_sc[...] = carry
    cnt_ref[...] = jnp.broadcast_to(carry, cnt_ref.shape)

    ti = jnp.zeros(logits.shape, F32)
    tw = jnp.zeros(logits.shape, F32)
    tr = jnp.zeros(logits.shape, F32)
    for k in range(TOP_K):
        rank = jnp.sum(jnp.where(lane == idxs[k], before, 0.0), axis=-1, keepdims=True)
        ti = jnp.where(lane == float(k), idxs[k], ti)
        tw = jnp.where(lane == float(k), es[k] / den, tw)
        tr = jnp.where(lane == float(k), rank, tr)
    ti_ref[...] = ti.astype(I32)
    tw_ref[...] = tw
    tr_ref[...] = tr.astype(I32)


def _outproj(merged, wo, layer, x, g1, ng, sc, sh, rw, rb, seq):
    t, d = x.shape
    bm = 512
    tps = seq // bm
    row = lambda w: pl.BlockSpec((bm, w), lambda i: (i, 0))
    vec = pl.BlockSpec((None, 1, d), lambda i: (i // tps, 0, 0))
    full = lambda a: pl.BlockSpec(a.shape, lambda i: (0, 0))
    return pl.pallas_call(
        _outproj_kernel,
        grid=(t // bm,),
        in_specs=[row(d), pl.BlockSpec((None, d, d), lambda i: (layer, 0, 0)), row(d), vec, full(ng), vec, vec,
                  full(rw), full(rb)],
        out_specs=[row(d), pl.BlockSpec((bm * SUBLANES, LANES), lambda i: (i, 0)),
                   row(LANES), row(LANES), row(LANES),
                   pl.BlockSpec((SUBLANES, LANES), lambda i: (0, 0))],
        out_shape=[jax.ShapeDtypeStruct((t, d), F32), jax.ShapeDtypeStruct((t * SUBLANES, LANES), U32),
                   jax.ShapeDtypeStruct((t, LANES), I32), jax.ShapeDtypeStruct((t, LANES), F32),
                   jax.ShapeDtypeStruct((t, LANES), I32), jax.ShapeDtypeStruct((SUBLANES, LANES), F32)],
        scratch_shapes=[pltpu.VMEM((1, LANES), F32)],
        compiler_params=_params("arbitrary"),
        name="outproj_router",
    )(merged, wo, x, g1, ng, sc, sh, rw, rb)


def _row_copy(src_ref, src_row, dst_ref, dst_row, sem):
    return pltpu.make_async_copy(src_ref.at[pl.ds(pl.multiple_of(src_row * SUBLANES, SUBLANES), SUBLANES), :],
                                 dst_ref.at[pl.ds(pl.multiple_of(dst_row * SUBLANES, SUBLANES), SUBLANES), :],
                                 sem)


def _dispatch_kernel(pad_lo_ref, pad_hi_ref, nu_ref, dest_ref, hp_ref, xb_hbm, zeros_sc, sem, zsem,
                     *, bm, n_blk):
    blk_rows = MOE_BLOCK * SUBLANES

    @pl.when(pl.program_id(0) == 0)
    def _():
        zeros_sc[...] = jnp.zeros(zeros_sc.shape, zeros_sc.dtype)

        def pad_rows(act):
            def per_expert(e, carry):
                row = pad_lo_ref[e]
                n_pad = pad_hi_ref[e] - row
                p = MOE_BLOCK // 2
                while p >= 1:
                    take = (n_pad & p) != 0

                    @pl.when(take)
                    def _(row=row, p=p):
                        dst = xb_hbm.at[pl.ds(pl.multiple_of(row * SUBLANES, SUBLANES), p * SUBLANES), :]
                        act(pltpu.make_async_copy(zeros_sc.at[pl.ds(0, p * SUBLANES), :], dst, zsem))

                    row = row + jnp.where(take, p, 0)
                    p //= 2
                return carry
            lax.fori_loop(0, N_EXPERTS, per_expert, 0)

        def tail_blocks(act):
            def per_block(b, carry):
                off = pl.multiple_of(b * blk_rows, blk_rows)
                act(pltpu.make_async_copy(zeros_sc, xb_hbm.at[pl.ds(off, blk_rows), :], zsem))
                return carry
            lax.fori_loop(nu_ref[0], n_blk, per_block, 0)

        pad_rows(lambda cp: cp.start())
        tail_blocks(lambda cp: cp.start())
        pad_rows(lambda cp: cp.wait())
        tail_blocks(lambda cp: cp.wait())

    def issue(r, carry):
        for k in range(TOP_K):
            _row_copy(hp_ref, r, xb_hbm, dest_ref[0, 0, r * TOP_K + k], sem).start(priority=k % 2)
        return carry

    lax.fori_loop(0, bm, issue, 0, unroll=4)
    for k in range(TOP_K):
        pltpu.make_async_copy(hp_ref, xb_hbm.at[pl.ds(0, bm * SUBLANES), :], sem).wait()


def _dispatch(pad_lo, pad_hi, n_used, dest, hp, n_rows):
    t = dest.shape[0]
    bm = 512
    nt = t // bm
    n_blk = n_rows // MOE_BLOCK
    return pl.pallas_call(
        functools.partial(_dispatch_kernel, bm=bm, n_blk=n_blk),
        grid_spec=pltpu.PrefetchScalarGridSpec(
            num_scalar_prefetch=3,
            grid=(nt,),
            in_specs=[pl.BlockSpec((1, 1, bm * TOP_K), lambda i, *_: (i, 0, 0), memory_space=pltpu.SMEM),
                      pl.BlockSpec((bm * SUBLANES, LANES), lambda i, *_: (i, 0))],
            out_specs=pl.BlockSpec(memory_space=pl.ANY),
            scratch_shapes=[pltpu.VMEM((MOE_BLOCK * SUBLANES, LANES), U32),
                            pltpu.SemaphoreType.DMA(()), pltpu.SemaphoreType.DMA(())]),
        out_shape=jax.ShapeDtypeStruct((n_rows * SUBLANES, LANES), U32),
        compiler_params=_params("arbitrary"),
        name="moe_dispatch",
    )(pad_lo, pad_hi, n_used, dest.reshape(nt, 1, bm * TOP_K), hp)


def _expert_kernel(be_ref, nu_ref, xb_ref, wgu_ref, bgu_ref, wdn_ref, bdn_ref, o_ref):
    b = pl.program_id(0)
    rows = MOE_BLOCK

    @pl.when(b < nu_ref[0])
    def _():
        los, his = [], []
        for c in range(SUBLANES):
            lo, hi = _unpack_pair(_load_row_tile_col(xb_ref, c, rows))
            los.append(lo.astype(BF16))
            his.append(hi.astype(BF16))
        lo = jnp.concatenate(los, axis=1)
        hi = jnp.concatenate(his, axis=1)
        half = lo.shape[1]
        gu = (jnp.dot(lo, wgu_ref[:half, :], preferred_element_type=F32)
              + jnp.dot(hi, wgu_ref[half:, :], preferred_element_type=F32) + bgu_ref[...])
        g = jnp.minimum(gu[:, :D_EXPERT], SWIGLU_LIMIT)
        u = jnp.clip(gu[:, D_EXPERT:], -SWIGLU_LIMIT, SWIGLU_LIMIT)
        act = (u + 1.0) * (g * jax.nn.sigmoid(SWIGLU_ALPHA * g))
        out = jnp.dot(act.astype(BF16), wdn_ref[...], preferred_element_type=F32) + bdn_ref[...]
        _store_row_tiles(o_ref, _pack_pair(out[:, :half], out[:, half:]))

    @pl.when(b >= nu_ref[0])
    def _():
        o_ref[...] = jnp.zeros(o_ref.shape, o_ref.dtype)


def _experts(block_e, n_used, xb, w_gu, b_gu, w_dn, b_dn):
    n_blk = xb.shape[0] // (MOE_BLOCK * SUBLANES)
    ne, d, _ = w_gu.shape
    blk = pl.BlockSpec((MOE_BLOCK * SUBLANES, LANES), lambda b, be, nu: (b, 0))
    blk_in = pl.BlockSpec((MOE_BLOCK * SUBLANES, LANES), lambda b, be, nu: (jnp.minimum(b, nu[0] - 1), 0))
    return pl.pallas_call(
        _expert_kernel,
        grid_spec=pltpu.PrefetchScalarGridSpec(
            num_scalar_prefetch=2,
            grid=(n_blk,),
            in_specs=[blk_in,
                      pl.BlockSpec((None, d, 2 * D_EXPERT), lambda b, be, nu: (be[b], 0, 0)),
                      pl.BlockSpec((None, 1, 2 * D_EXPERT), lambda b, be, nu: (be[b], 0, 0)),
                      pl.BlockSpec((None, D_EXPERT, d), lambda b, be, nu: (be[b], 0, 0)),
                      pl.BlockSpec((None, 1, d), lambda b, be, nu: (be[b], 0, 0))],
            out_specs=blk),
        out_shape=jax.ShapeDtypeStruct(xb.shape, U32),
        compiler_params=_params("arbitrary"),
        name="moe_experts",
    )(block_e, n_used, xb, w_gu, b_gu.reshape(ne, 1, -1), w_dn, b_dn.reshape(ne, 1, -1))


def _combine_kernel(dest_ref, dnext_ref, tw_ref, x_ref, g2_ref, ng_ref, sc_ref, sh_ref, outp_hbm, *rest,
                    bm, final):
    if final:
        ho_ref, buf, sem = rest
    else:
        xo_ref, ho_ref, buf, sem = rest
    i = pl.program_id(0)
    slot = i % 2

    def gather_tile(d_ref, s):
        def issue(r, carry):
            for k in range(TOP_K):
                _row_copy(outp_hbm, d_ref[0, 0, r * TOP_K + k], buf.at[s, k], r, sem.at[s]).start(priority=k % 2)
            return carry

        lax.fori_loop(0, bm, issue, 0, unroll=4)

    @pl.when(i == 0)
    def _():
        gather_tile(dest_ref, 0)

    @pl.when(i + 1 < pl.num_programs(0))
    def _():
        gather_tile(dnext_ref, 1 - slot)

    for k in range(TOP_K):
        pltpu.make_async_copy(outp_hbm.at[pl.ds(0, bm * SUBLANES), :], buf.at[slot, k], sem.at[slot]).wait()

    tw = tw_ref[...]
    wk = [tw[:, k:k + 1] for k in range(TOP_K)]
    y_lo, y_hi = [], []
    for c in range(SUBLANES):
        a_lo = a_hi = None
        for k in range(TOP_K):
            lo, hi = _unpack_pair(buf[slot, k, pl.ds(c, bm, stride=SUBLANES), :])
            a_lo = wk[k] * lo if a_lo is None else a_lo + wk[k] * lo
            a_hi = wk[k] * hi if a_hi is None else a_hi + wk[k] * hi
        y_lo.append(a_lo)
        y_hi.append(a_hi)
    x_new = x_ref[...] + g2_ref[...] * jnp.concatenate(y_lo + y_hi, axis=1)
    if final:
        ho_ref[...] = x_new * lax.rsqrt(jnp.mean(x_new * x_new, axis=-1, keepdims=True) + EPS) * ng_ref[...]
    else:
        xo_ref[...] = x_new
        ho_ref[...] = _rms_mod(x_new, ng_ref[...], sc_ref[...], sh_ref[...]).astype(BF16)


def _combine(dest, tw, x, g2, ng, sc, sh, outp, seq, final):
    t, d = x.shape
    bm = 256
    tps = seq // bm
    nt = t // bm
    row = lambda w: pl.BlockSpec((bm, w), lambda i: (i, 0))
    vec = pl.BlockSpec((None, 1, d), lambda i: (i // tps, 0, 0))
    dest3 = dest.reshape(nt, 1, bm * TOP_K)
    if final:
        out_specs = [row(d)]
        out_shape = [jax.ShapeDtypeStruct((t, d), F32)]
    else:
        out_specs = [row(d), row(d)]
        out_shape = [jax.ShapeDtypeStruct((t, d), F32), jax.ShapeDtypeStruct((t, d), BF16)]
    return pl.pallas_call(
        functools.partial(_combine_kernel, bm=bm, final=final),
        grid=(nt,),
        in_specs=[pl.BlockSpec((1, 1, bm * TOP_K), lambda i: (i, 0, 0), memory_space=pltpu.SMEM),
                  pl.BlockSpec((1, 1, bm * TOP_K), lambda i: (jnp.minimum(i + 1, nt - 1), 0, 0),
                               memory_space=pltpu.SMEM),
                  row(LANES), row(d), vec, pl.BlockSpec((1, d), lambda i: (0, 0)), vec, vec,
                  pl.BlockSpec(memory_space=pl.ANY)],
        out_specs=out_specs,
        out_shape=out_shape,
        scratch_shapes=[pltpu.VMEM((2, TOP_K, bm * SUBLANES, LANES), U32), pltpu.SemaphoreType.DMA((2,))],
        compiler_params=_params("arbitrary"),
        name="moe_combine",
    )(dest3, dest3, tw, x, g2, ng, sc, sh, outp)


def _routing_tables(top_i, rank, counts_f):
    t = top_i.shape[0]
    n_asg = t * TOP_K
    counts = counts_f.astype(I32)
    padded = ((counts + MOE_BLOCK - 1) // MOE_BLOCK) * MOE_BLOCK
    pend = jnp.cumsum(padded)
    pstart = pend - padded
    experts = jnp.arange(N_EXPERTS, dtype=I32)
    start_of = jnp.sum(jnp.where(top_i[:, :, None] == experts, pstart, 0), axis=-1)
    dest = (start_of + rank).astype(I32)
    n_rows = ((n_asg + N_EXPERTS * MOE_BLOCK + MOE_BLOCK - 1) // MOE_BLOCK) * MOE_BLOCK
    n_blk = n_rows // MOE_BLOCK
    first_row = jnp.arange(n_blk, dtype=I32) * MOE_BLOCK
    block_e = jnp.minimum(jnp.sum((pend[None, :] <= first_row[:, None]).astype(I32), axis=1), N_EXPERTS - 1)
    n_used = (pend[-1:] // MOE_BLOCK).astype(I32)
    return dest, block_e, n_used, n_rows, (pstart + counts).astype(I32), pend.astype(I32)


def _rotary_tables(positions):
    inv_freq = ROPE_THETA ** (-jnp.arange(0, ROT_DIM, 2, dtype=F32) / ROT_DIM)
    ang = positions.astype(F32).reshape(-1, 1) * inv_freq
    cos, sin = jnp.cos(ang), jnp.sin(ang)
    t = ang.shape[0]
    pad = jnp.zeros((t, DA_DK - ROT_DIM), F32)
    c64 = jnp.concatenate([cos, cos, pad + 1.0], axis=1)
    s1_64 = jnp.concatenate([jnp.zeros_like(sin), sin, pad], axis=1)
    s2_64 = jnp.concatenate([-sin, jnp.zeros_like(sin), pad], axis=1)
    rep = LANES // DA_DK
    return jnp.tile(c64, (1, rep)), jnp.tile(s1_64, (1, rep)), jnp.tile(s2_64, (1, rep)), cos.T, sin.T


def kernel(x, c, positions, w_ada, b_ada, norm_mix, norm_ffn, w_in, diff_lambda, diff_subln, gmlp_ln_g, gmlp_ln_b, gmlp_w_spatial, gmlp_b_spatial, pool_w, pool_scale, conv_w, w_branch, w_gate, b_gate, w_out, router_w, router_b, expert_w_gu, expert_b_gu, expert_w_down, expert_b_down, final_norm):
    bsz, seq, d = x.shape
    depth = w_ada.shape[0]
    t = bsz * seq
    assert seq % 512 == 0 and d == 2 * MIX_W and d // 2 == SUBLANES * LANES

    rot_c, rot_s1, rot_s2, cos_t, sin_t = _rotary_tables(positions)
    c_pad = jnp.zeros((SUBLANES, d), F32).at[:bsz].set(c)
    mod = _adaln(c_pad, w_ada, b_ada)[:, :bsz].reshape(depth, bsz, 6, 1, d)
    rw_pad = jnp.zeros((depth, d, LANES), F32).at[:, :, :N_EXPERTS].set(router_w)
    rb_pad = jnp.full((depth, 1, LANES), NEG_INF, F32).at[:, 0, :N_EXPERTS].set(router_b)

    w_in_b, w_gate_b, w_branch_b, w_out_b = (w.astype(BF16) for w in (w_in, w_gate, w_branch, w_out))

    xf = x.reshape(t, d)
    h = None
    for l in range(depth):
        lam_init = 0.8 - 0.6 * math.exp(-0.3 * l)
        sh1, sc1, g1, sh2, sc2, g2 = (mod[l, :, i] for i in range(6))
        if l == 0:
            h = _norm_mod(xf, norm_mix[l].reshape(1, d), sc1, sh1, seq)
        z, qt, vt = _inproj(h, w_in_b, l, rot_c, rot_s1, rot_s2, cos_t, sin_t)
        y_a, w_gu_b, w_dn_b = _attention(z, qt, vt, diff_lambda[l], diff_subln[l].reshape(DA_DV, 1),
                                         expert_w_gu, expert_w_down, l, bsz, seq, lam_init)
        bs_b = jnp.broadcast_to(gmlp_b_spatial[l][:, :, None], (GM_GROUPS, GM_CHUNK, GM_CHUNK))
        y_b = _gmlp(z, gmlp_ln_g[l].reshape(1, MIX_W), gmlp_ln_b[l].reshape(1, MIX_W),
                    gmlp_w_spatial[l], bs_b)
        y_c = _pool(z, pool_w[l].astype(BF16), pool_scale[l].reshape(1, MIX_W), seq)
        y_d = _conv(z, conv_w[l], seq)
        merged = _merge(h, (y_a, y_b, y_c, y_d), w_gate_b, b_gate[l], w_branch_b, l)
        xf, hp, ti, tw, tr, cnt = _outproj(merged, w_out_b, l, xf, g1, norm_ffn[l].reshape(1, d),
                                           sc2, sh2, rw_pad[l], rb_pad[l], seq)
        dest, block_e, n_used, n_rows, pad_lo, pad_hi = _routing_tables(ti[:, :TOP_K], tr[:, :TOP_K],
                                                                        cnt[0, :N_EXPERTS])
        xb = _dispatch(pad_lo, pad_hi, n_used, dest, hp, n_rows)
        outp = _experts(block_e, n_used, xb, w_gu_b, expert_b_gu[l], w_dn_b, expert_b_down[l])
        if l + 1 < depth:
            nsh1, nsc1 = mod[l + 1, :, 0], mod[l + 1, :, 1]
            xf, h = _combine(dest, tw, xf, g2, norm_mix[l + 1].reshape(1, d), nsc1, nsh1, outp, seq, False)
        else:
            zero = jnp.zeros((bsz, 1, d), F32)
            (out,) = _combine(dest, tw, xf, g2, final_norm.reshape(1, d), zero, zero, outp, seq, True)
    return out.reshape(bsz, seq, d)
```

```python
import functools
import math

import jax
import jax.numpy as jnp
from jax import lax
from jax.experimental import pallas as pl
from jax.experimental.pallas import tpu as pltpu

F32 = jnp.float32
BF16 = jnp.bfloat16
U32 = jnp.uint32
I32 = jnp.int32

MIX_W = 1024
DA_HEADS = 8
DA_DK = 64
DA_DV = 128
ROT_DIM = 16
ROPE_THETA = 500000.0
GM_GROUPS = 8
GM_CHUNK = 128
POOL_WINDOWS = (2, 4, 8, 16)
POOL_GC = 256
N_BRANCH = 4
N_EXPERTS = 32
TOP_K = 4
D_EXPERT = 1024
SWIGLU_LIMIT = 7.0
SWIGLU_ALPHA = 1.702
MOE_BLOCK = 512
EPS = 1e-6
NEG_INF = -1e30

LANES = 128
SUBLANES = 8
HALO = 16
ONES_ROWS = 16
VMEM_LIMIT = 56 * 1024 * 1024

COL_K, COL_GU, COL_GV, COL_POOL, COL_CB, COL_CC, COL_CH = range(7)
IN_Q, IN_K, IN_V = 0, 1, 2


def _params(*sem):
    return pltpu.CompilerParams(dimension_semantics=sem, vmem_limit_bytes=VMEM_LIMIT)


def _rms_mod(x, g, sc, sh):
    y = x * lax.rsqrt(jnp.mean(x * x, axis=-1, keepdims=True) + EPS) * g
    return y * (1.0 + sc) + sh


def _pack_pair(lo, hi):
    lo_bits = lax.bitcast_convert_type(lo.astype(BF16).astype(F32), U32) >> 16
    hi_bits = lax.bitcast_convert_type(hi.astype(BF16).astype(F32), U32) & jnp.uint32(0xFFFF0000)
    return hi_bits | lo_bits


def _unpack_pair(w):
    lo = lax.bitcast_convert_type(w << 16, F32)
    hi = lax.bitcast_convert_type(w & jnp.uint32(0xFFFF0000), F32)
    return lo, hi


def _store_row_tiles(ref, packed):
    n = packed.shape[0]
    for c in range(SUBLANES):
        ref[pl.ds(c, n, stride=SUBLANES), :] = packed[:, c * LANES:(c + 1) * LANES]


def _load_row_tile_col(ref, c, n):
    return ref[pl.ds(c, n, stride=SUBLANES), :]


def _adaln_kernel(c_ref, w_ref, b_ref, o_ref):
    c = c_ref[...]
    o_ref[...] = jnp.dot(c * jax.nn.sigmoid(c), w_ref[...], preferred_element_type=F32) + b_ref[...]


def _adaln(c_pad, w_ada, b_ada):
    depth, d, n = w_ada.shape
    bn = 1536
    rows = c_pad.shape[0]
    return pl.pallas_call(
        _adaln_kernel,
        grid=(depth, n // bn),
        in_specs=[pl.BlockSpec((rows, d), lambda l, j: (0, 0)),
                  pl.BlockSpec((None, d, bn), lambda l, j: (l, 0, j)),
                  pl.BlockSpec((None, 1, bn), lambda l, j: (l, 0, j))],
        out_specs=pl.BlockSpec((None, rows, bn), lambda l, j: (l, 0, j)),
        out_shape=jax.ShapeDtypeStruct((depth, rows, n), F32),
        compiler_params=_params("arbitrary", "arbitrary"),
        name="adaln",
    )(c_pad, w_ada, b_ada.reshape(depth, 1, n))


def _norm_kernel(x_ref, g_ref, sc_ref, sh_ref, h_ref):
    h_ref[...] = _rms_mod(x_ref[...], g_ref[...], sc_ref[...], sh_ref[...]).astype(BF16)


def _norm_mod(x, g, sc, sh, seq):
    t, d = x.shape
    bm = 512
    tps = seq // bm
    vec = pl.BlockSpec((None, 1, d), lambda i: (i // tps, 0, 0))
    return pl.pallas_call(
        _norm_kernel,
        grid=(t // bm,),
        in_specs=[pl.BlockSpec((bm, d), lambda i: (i, 0)),
                  pl.BlockSpec((1, d), lambda i: (0, 0)), vec, vec],
        out_specs=pl.BlockSpec((bm, d), lambda i: (i, 0)),
        out_shape=jax.ShapeDtypeStruct((t, d), BF16),
        compiler_params=_params("arbitrary"),
        name="norm_mod",
    )(x, g, sc, sh)


def _inproj_kernel(h_ref, w_ref, c_ref, s1_ref, s2_ref, ct_ref, st_ref, z_ref, qt_ref, vt_ref):
    j = pl.program_id(1)
    acc = jnp.dot(h_ref[...], w_ref[...], preferred_element_type=F32)
    n_tiles = acc.shape[1] // LANES
    half = ROT_DIM // 2
    z_ref[...] = acc.astype(BF16)

    @pl.when(j == IN_Q)
    def _():
        scale = DA_DK ** -0.5 * math.log2(math.e)
        cos = ct_ref[...] * scale
        sin = st_ref[...] * scale
        for t in range(n_tiles):
            xt = acc[:, t * LANES:(t + 1) * LANES].T
            parts = []
            for comp in range(LANES // DA_DK):
                base = comp * DA_DK
                x1 = xt[base:base + half]
                x2 = xt[base + half:base + ROT_DIM]
                parts += [x1 * cos - x2 * sin, x2 * cos + x1 * sin, xt[base + ROT_DIM:base + DA_DK] * scale]
            qt_ref[t * LANES:(t + 1) * LANES, :] = jnp.concatenate(parts, axis=0).astype(BF16)

    @pl.when(j == IN_K)
    def _():
        for t in range(n_tiles):
            xt = acc[:, t * LANES:(t + 1) * LANES]
            r = (xt * c_ref[...] + pltpu.roll(xt, half, 1) * s1_ref[...]
                 + pltpu.roll(xt, LANES - half, 1) * s2_ref[...])
            z_ref[:, t * LANES:(t + 1) * LANES] = r.astype(BF16)

    @pl.when(j == IN_V)
    def _():
        for t in range(n_tiles):
            vt_ref[t * LANES:(t + 1) * LANES, :] = acc[:, t * LANES:(t + 1) * LANES].T.astype(BF16)


def _inproj(h, w_in, layer, rot_c, rot_s1, rot_s2, cos_t, sin_t):
    t, d = h.shape
    n = w_in.shape[2]
    bm, bn = 1024, MIX_W
    nz = n // bn - 2
    rot = pl.BlockSpec((bm, LANES), lambda i, j: (i, 0))
    rot_t = pl.BlockSpec((ROT_DIM // 2, bm), lambda i, j: (0, i))
    z_col = lambda j: jnp.where(j <= IN_K, 0, jnp.maximum(j - 2, 1))
    return pl.pallas_call(
        _inproj_kernel,
        grid=(t // bm, n // bn),
        in_specs=[pl.BlockSpec((bm, d), lambda i, j: (i, 0)),
                  pl.BlockSpec((None, d, bn), lambda i, j: (layer, 0, j)), rot, rot, rot, rot_t, rot_t],
        out_specs=[pl.BlockSpec((bm, bn), lambda i, j: (i, z_col(j))),
                   pl.BlockSpec((bn, bm), lambda i, j: (0, i)),
                   pl.BlockSpec((bn, bm), lambda i, j: (0, i))],
        out_shape=[jax.ShapeDtypeStruct((t, nz * bn), BF16),
                   jax.ShapeDtypeStruct((bn, t), BF16),
                   jax.ShapeDtypeStruct((bn, t), BF16)],
        compiler_params=_params("arbitrary", "arbitrary"),
        name="inproj",
    )(h, w_in, rot_c, rot_s1, rot_s2, cos_t, sin_t)


def _attn_kernel(qt_ref, k_ref, vt_ref, lam_ref, sub_ref, wgu_ref, wdn_ref, o_ref, wgu_o_ref, wdn_o_ref,
                 sa_ref, sb_ref, ma_sc, mb_sc, m_sc, acc_sc, *, tq, lam_init):
    i = pl.program_id(2)
    wgu_o_ref[...] = wgu_ref[...].astype(BF16)
    wdn_o_ref[...] = wdn_ref[...].astype(BF16)
    qt = qt_ref[...]
    row = lax.broadcasted_iota(I32, qt.shape, 0)
    zero = jnp.zeros_like(qt)
    qs = jnp.concatenate([jnp.where(row < DA_DK, qt, zero), jnp.where(row >= DA_DK, qt, zero)], axis=1)
    m_sc[...] = jnp.full(m_sc.shape, NEG_INF, F32)
    acc_sc[...] = jnp.zeros(acc_sc.shape, F32)

    def scores(j, s_ref, smax_ref):
        off = pl.multiple_of(j * tq, tq)
        s = jnp.dot(k_ref[pl.ds(off, tq), :], qs, preferred_element_type=F32)
        s_ref[...] = s
        smax_ref[...] = jnp.max(s, axis=0, keepdims=True)

    def softmax_pv(j, s_ref, smax_ref, masked):
        off = pl.multiple_of(j * tq, tq)
        s = s_ref[...]
        if masked:
            key = lax.broadcasted_iota(I32, s.shape, 0)
            c2 = lax.broadcasted_iota(I32, s.shape, 1)
            s = jnp.where(key <= jnp.where(c2 >= tq, c2 - tq, c2), s, NEG_INF)
            s_max = jnp.max(s, axis=0, keepdims=True)
        else:
            s_max = smax_ref[...]
        m_prev = m_sc[...]
        m_new = jnp.maximum(m_prev, s_max)
        alpha = jnp.exp2(m_prev - m_new)
        p = jnp.exp2(s - m_new).astype(BF16)
        vt_aug = jnp.concatenate([vt_ref[:, pl.ds(off, tq)], jnp.ones((ONES_ROWS, tq), BF16)], axis=0)
        acc_sc[...] = alpha * acc_sc[...] + jnp.dot(vt_aug, p, preferred_element_type=F32)
        m_sc[...] = m_new

    scores(0, sa_ref, ma_sc)

    def pair(jj, carry):
        j = 2 * jj
        scores(j + 1, sb_ref, mb_sc)
        softmax_pv(j, sa_ref, ma_sc, False)
        scores(j + 2, sa_ref, ma_sc)
        softmax_pv(j + 1, sb_ref, mb_sc, False)
        return carry

    lax.fori_loop(0, i // 2, pair, 0)

    @pl.when(i % 2 == 1)
    def _():
        scores(i, sb_ref, mb_sc)
        softmax_pv(i - 1, sa_ref, ma_sc, False)
        softmax_pv(i, sb_ref, mb_sc, True)

    @pl.when(i % 2 == 0)
    def _():
        softmax_pv(i, sa_ref, ma_sc, True)

    lv = lam_ref[...]
    lam = (jnp.exp(jnp.sum(lv[0:1] * lv[1:2], axis=-1, keepdims=True))
           - jnp.exp(jnp.sum(lv[2:3] * lv[3:4], axis=-1, keepdims=True)) + lam_init)
    acc = acc_sc[:DA_DV, :]
    l = acc_sc[DA_DV:DA_DV + 1, :]
    o = acc[:, :tq] / l[:, :tq] - lam * (acc[:, tq:] / l[:, tq:])
    y = o * lax.rsqrt(jnp.mean(o * o, axis=0, keepdims=True) + EPS) * sub_ref[...] * (1.0 - lam_init)
    o_ref[...] = y.T.astype(o_ref.dtype)


def _attention(z, qt, vt, lam_vecs, subln_col, w_gu, w_dn, layer, bsz, seq, lam_init):
    t = z.shape[0]
    tq = min(512, seq)
    nq = seq // tq
    steps = bsz * DA_HEADS * nq
    _, ne, d, gu_w = w_gu.shape
    gu_rows, dn_rows = ne * d // steps, ne * D_EXPERT // steps
    assert gu_rows * steps == ne * d and dn_rows * steps == ne * D_EXPERT and dn_rows % HALO == 0
    step = lambda b, h, i: (b * DA_HEADS + h) * nq + i
    y_a, gu_b, dn_b = pl.pallas_call(
        functools.partial(_attn_kernel, tq=tq, lam_init=lam_init),
        grid=(bsz, DA_HEADS, nq),
        in_specs=[pl.BlockSpec((DA_DV, tq), lambda b, h, i: (h, b * nq + i)),
                  pl.BlockSpec((seq, DA_DV), lambda b, h, i: (b, COL_K * DA_HEADS + h)),
                  pl.BlockSpec((DA_DV, seq), lambda b, h, i: (h, b)),
                  pl.BlockSpec((4, DA_DK), lambda b, h, i: (0, 0)),
                  pl.BlockSpec((DA_DV, 1), lambda b, h, i: (0, 0)),
                  pl.BlockSpec((None, gu_rows, gu_w), lambda b, h, i: (layer, step(b, h, i), 0)),
                  pl.BlockSpec((None, dn_rows, d), lambda b, h, i: (layer, step(b, h, i), 0))],
        out_specs=[pl.BlockSpec((tq, DA_DV), lambda b, h, i: (b * nq + i, h)),
                   pl.BlockSpec((gu_rows, gu_w), lambda b, h, i: (step(b, h, i), 0)),
                   pl.BlockSpec((dn_rows, d), lambda b, h, i: (step(b, h, i), 0))],
        out_shape=[jax.ShapeDtypeStruct((t, MIX_W), BF16),
                   jax.ShapeDtypeStruct((ne * d, gu_w), BF16),
                   jax.ShapeDtypeStruct((ne * D_EXPERT, d), BF16)],
        scratch_shapes=[pltpu.VMEM((tq, 2 * tq), F32), pltpu.VMEM((tq, 2 * tq), F32),
                        pltpu.VMEM((1, 2 * tq), F32), pltpu.VMEM((1, 2 * tq), F32),
                        pltpu.VMEM((1, 2 * tq), F32), pltpu.VMEM((DA_DV + ONES_ROWS, 2 * tq), F32)],
        compiler_params=_params("arbitrary", "arbitrary", "arbitrary"),
        name="diff_attn",
    )(qt, z, vt, lam_vecs, subln_col, w_gu.reshape(w_gu.shape[0], ne * d, gu_w),
      w_dn.reshape(w_dn.shape[0], ne * D_EXPERT, d))
    return y_a, gu_b.reshape(ne, d, gu_w), dn_b.reshape(ne, D_EXPERT, d)


def _gelu(x):
    return 0.5 * x * (1.0 + lax.erf(x * math.sqrt(0.5)))


def _gmlp_kernel(u_ref, v_ref, g_ref, b_ref, ws_ref, bs_ref, o_ref, *, bm):
    v = _gelu(v_ref[...].astype(F32))
    mu = jnp.mean(v, axis=-1, keepdims=True)
    vc = v - mu
    var = jnp.mean(vc * vc, axis=-1, keepdims=True)
    vn = (vc * lax.rsqrt(var + EPS) * g_ref[...] + b_ref[...]).astype(BF16)
    r = lax.broadcasted_iota(I32, (GM_CHUNK, GM_CHUNK), 0)
    c = lax.broadcasted_iota(I32, (GM_CHUNK, GM_CHUNK), 1)
    for g in range(GM_GROUPS):
        cols = slice(g * GM_CHUNK, (g + 1) * GM_CHUNK)
        wm = jnp.where(c <= r, ws_ref[g], 0.0).astype(BF16)
        bias = bs_ref[g]
        for n in range(bm // GM_CHUNK):
            rows = slice(n * GM_CHUNK, (n + 1) * GM_CHUNK)
            sv = jnp.dot(wm, vn[rows, cols], preferred_element_type=F32) + bias
            u = _gelu(u_ref[rows, cols].astype(F32))
            o_ref[rows, cols] = (u * sv).astype(o_ref.dtype)


def _gmlp(z, ln_g, ln_b, ws, bs_b):
    t = z.shape[0]
    bm = 512
    full = lambda shape: pl.BlockSpec(shape, lambda i: (0,) * len(shape))
    return pl.pallas_call(
        functools.partial(_gmlp_kernel, bm=bm),
        grid=(t // bm,),
        in_specs=[pl.BlockSpec((bm, MIX_W), lambda i: (i, COL_GU)),
                  pl.BlockSpec((bm, MIX_W), lambda i: (i, COL_GV)),
                  full((1, MIX_W)), full((1, MIX_W)),
                  full((GM_GROUPS, GM_CHUNK, GM_CHUNK)), full((GM_GROUPS, GM_CHUNK, GM_CHUNK))],
        out_specs=pl.BlockSpec((bm, MIX_W), lambda i: (i, 0)),
        out_shape=jax.ShapeDtypeStruct((t, MIX_W), BF16),
        compiler_params=_params("arbitrary"),
        name="gmlp",
    )(z, z, ln_g, ln_b, ws, bs_b)


def _pool_kernel(z_ref, halo_ref, pw_ref, ps_ref, o_ref, band_sc, bandh_sc, *, bm, tps):
    p0 = (pl.program_id(0) % tps) * bm

    @pl.when(pl.program_id(0) == 0)
    def _():
        d = (lax.broadcasted_iota(I32, (bm, bm), 0) - lax.broadcasted_iota(I32, (bm, bm), 1))
        dh = (lax.broadcasted_iota(I32, (bm, HALO), 0) + HALO - lax.broadcasted_iota(I32, (bm, HALO), 1))
        for g, w in enumerate(POOL_WINDOWS):
            band_sc[g] = jnp.where(d >= 0, jnp.where(d < w, 1.0, 0.0), 0.0).astype(BF16)
            bandh_sc[g] = jnp.where(dh < w, 1.0, 0.0).astype(BF16)

    pos = p0 + lax.broadcasted_iota(I32, (bm, 1), 0)
    for g, w in enumerate(POOL_WINDOWS):
        cols = slice(g * POOL_GC, (g + 1) * POOL_GC)
        x = z_ref[:, cols]
        hx = halo_ref[:, cols]
        hx = jnp.where(p0 > 0, hx, jnp.zeros_like(hx))
        win = (jnp.dot(band_sc[g], x, preferred_element_type=F32)
               + jnp.dot(bandh_sc[g], hx, preferred_element_type=F32))
        cnt = jnp.minimum(pos + 1, w).astype(F32)
        pooled = (win / cnt - x.astype(F32)).astype(BF16)
        mixed = jnp.dot(pooled, pw_ref[g], preferred_element_type=F32)
        o_ref[:, cols] = (mixed * ps_ref[:, cols]).astype(o_ref.dtype)


def _halo_spec(bm, col):
    return pl.BlockSpec((HALO, MIX_W), lambda i: (jnp.maximum(i * (bm // HALO) - 1, 0), col))


def _pool(z, pw, ps, seq):
    t = z.shape[0]
    bm = min(512, seq)
    ng = len(POOL_WINDOWS)
    return pl.pallas_call(
        functools.partial(_pool_kernel, bm=bm, tps=seq // bm),
        grid=(t // bm,),
        in_specs=[pl.BlockSpec((bm, MIX_W), lambda i: (i, COL_POOL)),
                  _halo_spec(bm, COL_POOL),
                  pl.BlockSpec((ng, POOL_GC, POOL_GC), lambda i: (0, 0, 0)),
                  pl.BlockSpec((1, MIX_W), lambda i: (0, 0))],
        out_specs=pl.BlockSpec((bm, MIX_W), lambda i: (i, 0)),
        out_shape=jax.ShapeDtypeStruct((t, MIX_W), BF16),
        scratch_shapes=[pltpu.VMEM((ng, bm, bm), BF16), pltpu.VMEM((ng, bm, HALO), BF16)],
        compiler_params=_params("arbitrary"),
        name="pool",
    )(z, z, pw, ps)


def _conv_kernel(b_ref, c_ref, h_ref, hc_ref, hh_ref, cw_ref, o_ref, *, tps):
    first = (pl.program_id(0) % tps) == 0
    xin = c_ref[...].astype(F32) * h_ref[...].astype(F32)
    hal = hc_ref[...].astype(F32) * hh_ref[...].astype(F32)
    hal = jnp.where(first, 0.0, hal)
    hm1 = hal[HALO - 1:HALO, :]
    hm2 = hal[HALO - 2:HALO - 1, :]
    row = lax.broadcasted_iota(I32, xin.shape, 0)
    s1 = jnp.where(row == 0, hm1, pltpu.roll(xin, 1, 0))
    s2 = jnp.where(row == 0, hm2, jnp.where(row == 1, hm1, pltpu.roll(xin, 2, 0)))
    cw = cw_ref[...]
    y = cw[0:1] * s2 + cw[1:2] * s1 + cw[2:3] * xin
    o_ref[...] = (b_ref[...].astype(F32) * y).astype(o_ref.dtype)


def _conv(z, cw, seq):
    t = z.shape[0]
    bm = min(512, seq)
    blk = lambda col: pl.BlockSpec((bm, MIX_W), lambda i: (i, col))
    return pl.pallas_call(
        functools.partial(_conv_kernel, tps=seq // bm),
        grid=(t // bm,),
        in_specs=[blk(COL_CB), blk(COL_CC), blk(COL_CH), _halo_spec(bm, COL_CC), _halo_spec(bm, COL_CH),
                  pl.BlockSpec(cw.shape, lambda i: (0, 0))],
        out_specs=pl.BlockSpec((bm, MIX_W), lambda i: (i, 0)),
        out_shape=jax.ShapeDtypeStruct((t, MIX_W), BF16),
        compiler_params=_params("arbitrary"),
        name="short_conv",
    )(z, z, z, z, z, cw)


def _merge_kernel(h_ref, ya_ref, yb_ref, yc_ref, yd_ref, wg_ref, bg_ref, wb_ref, o_ref):
    h = h_ref[...]
    bg = bg_ref[...]
    merged = None
    for i, y_ref in enumerate((ya_ref, yb_ref, yc_ref, yd_ref)):
        gate = jax.nn.sigmoid(jnp.dot(h, wg_ref[i], preferred_element_type=F32) + bg[i:i + 1])
        term = gate * jnp.dot(y_ref[...], wb_ref[i], preferred_element_type=F32)
        merged = term if merged is None else merged + term
    o_ref[...] = merged.astype(o_ref.dtype)


def _merge(h, ys, wg, bg, wb, layer):
    t, d = h.shape
    bm, bn = 1024, 256
    yspec = pl.BlockSpec((bm, MIX_W), lambda i, j: (i, 0))
    return pl.pallas_call(
        _merge_kernel,
        grid=(t // bm, d // bn),
        in_specs=[pl.BlockSpec((bm, d), lambda i, j: (i, 0)), yspec, yspec, yspec, yspec,
                  pl.BlockSpec((None, N_BRANCH, d, bn), lambda i, j: (layer, 0, 0, j)),
                  pl.BlockSpec((N_BRANCH, bn), lambda i, j: (0, j)),
                  pl.BlockSpec((None, N_BRANCH, MIX_W, bn), lambda i, j: (layer, 0, 0, j))],
        out_specs=pl.BlockSpec((bm, bn), lambda i, j: (i, j)),
        out_shape=jax.ShapeDtypeStruct((t, d), BF16),
        compiler_params=_params("arbitrary", "arbitrary"),
        name="branch_merge",
    )(h, *ys, wg, bg, wb)


def _outproj_kernel(m_ref, wo_ref, x_ref, g1_ref, ng_ref, sc_ref, sh_ref, rw_ref, rb_ref,
                    xo_ref, hp_ref, ti_ref, tw_ref, tr_ref, cnt_ref, carry_sc):
    out = jnp.dot(m_ref[...], wo_ref[...], preferred_element_type=F32)
    x_new = x_ref[...] + g1_ref[...] * out
    xo_ref[...] = x_new
    h2 = _rms_mod(x_new, ng_ref[...], sc_ref[...], sh_ref[...])
    half = h2.shape[1] // 2
    _store_row_tiles(hp_ref, _pack_pair(h2[:, :half], h2[:, half:]))

    logits = jnp.dot(h2, rw_ref[...], preferred_element_type=F32) + rb_ref[...]
    lane = lax.broadcasted_iota(I32, logits.shape, 1).astype(F32)
    vals, idxs = [], []
    cur = logits
    for _ in range(TOP_K):
        m = jnp.max(cur, axis=-1, keepdims=True)
        idx = jnp.min(jnp.where(cur == m, lane, float(LANES)), axis=-1, keepdims=True)
        vals.append(m)
        idxs.append(idx)
        cur = jnp.where(lane == idx, -3e38, cur)
    es = [jnp.exp(v - vals[0]) for v in vals]
    den = es[0] + es[1] + es[2] + es[3]

    @pl.when(pl.program_id(0) == 0)
    def _():
        carry_sc[...] = jnp.zeros(carry_sc.shape, F32)

    onehot = jnp.zeros(logits.shape, F32)
    for k in range(TOP_K):
        onehot = onehot + jnp.where(lane == idxs[k], 1.0, 0.0)
    bm = logits.shape[0]
    tri = jnp.where(lax.broadcasted_iota(I32, (bm, bm), 0) > lax.broadcasted_iota(I32, (bm, bm), 1),
                    1.0, 0.0).astype(BF16)
    before = jnp.dot(tri, onehot.astype(BF16), preferred_element_type=F32) + carry_sc[...]
    carry = carry_sc[...] + jnp.sum(onehot, axis=0, keepdims=True)
    carry_sc[...] = carry
    cnt_ref[...] = jnp.broadcast_to(carry, cnt_ref.shape)

    ti = jnp.zeros(logits.shape, F32)
    tw = jnp.zeros(logits.shape, F32)
    tr = jnp.zeros(logits.shape, F32)
    for k in range(TOP_K):
        rank = jnp.sum(jnp.where(lane == idxs[k], before, 0.0), axis=-1, keepdims=True)
        ti = jnp.where(lane == float(k), idxs[k], ti)
        tw = jnp.where(lane == float(k), es[k] / den, tw)
        tr = jnp.where(lane == float(k), rank, tr)
    ti_ref[...] = ti.astype(I32)
    tw_ref[...] = tw
    tr_ref[...] = tr.astype(I32)


def _outproj(merged, wo, layer, x, g1, ng, sc, sh, rw, rb, seq):
    t, d = x.shape
    bm = 512
    tps = seq // bm
    row = lambda w: pl.BlockSpec((bm, w), lambda i: (i, 0))
    vec = pl.BlockSpec((None, 1, d), lambda i: (i // tps, 0, 0))
    full = lambda a: pl.BlockSpec(a.shape, lambda i: (0, 0))
    return pl.pallas_call(
        _outproj_kernel,
        grid=(t // bm,),
        in_specs=[row(d), pl.BlockSpec((None, d, d), lambda i: (layer, 0, 0)), row(d), vec, full(ng), vec, vec,
                  full(rw), full(rb)],
        out_specs=[row(d), pl.BlockSpec((bm * SUBLANES, LANES), lambda i: (i, 0)),
                   row(LANES), row(LANES), row(LANES),
                   pl.BlockSpec((SUBLANES, LANES), lambda i: (0, 0))],
        out_shape=[jax.ShapeDtypeStruct((t, d), F32), jax.ShapeDtypeStruct((t * SUBLANES, LANES), U32),
                   jax.ShapeDtypeStruct((t, LANES), I32), jax.ShapeDtypeStruct((t, LANES), F32),
                   jax.ShapeDtypeStruct((t, LANES), I32), jax.ShapeDtypeStruct((SUBLANES, LANES), F32)],
        scratch_shapes=[pltpu.VMEM((1, LANES), F32)],
        compiler_params=_params("arbitrary"),
        name="outproj_router",
    )(merged, wo, x, g1, ng, sc, sh, rw, rb)


def _row_copy(src_ref, src_row, dst_ref, dst_row, sem):
    return pltpu.make_async_copy(src_ref.at[pl.ds(pl.multiple_of(src_row * SUBLANES, SUBLANES), SUBLANES), :],
                                 dst_ref.at[pl.ds(pl.multiple_of(dst_row * SUBLANES, SUBLANES), SUBLANES), :],
                                 sem)


def _dispatch_kernel(pad_lo_ref, pad_hi_ref, nu_ref, dest_ref, hp_ref, xb_hbm, zeros_sc, sem, zsem,
                     *, bm, n_blk):
    blk_rows = MOE_BLOCK * SUBLANES

    @pl.when(pl.program_id(0) == 0)
    def _():
        zeros_sc[...] = jnp.zeros(zeros_sc.shape, zeros_sc.dtype)

        def pad_rows(act):
            def per_expert(e, carry):
                row = pad_lo_ref[e]
                n_pad = pad_hi_ref[e] - row
                p = MOE_BLOCK // 2
                while p >= 1:
                    take = (n_pad & p) != 0

                    @pl.when(take)
                    def _(row=row, p=p):
                        dst = xb_hbm.at[pl.ds(pl.multiple_of(row * SUBLANES, SUBLANES), p * SUBLANES), :]
                        act(pltpu.make_async_copy(zeros_sc.at[pl.ds(0, p * SUBLANES), :], dst, zsem))

                    row = row + jnp.where(take, p, 0)
                    p //= 2
                return carry
            lax.fori_loop(0, N_EXPERTS, per_expert, 0)

        def tail_blocks(act):
            def per_block(b, carry):
                off = pl.multiple_of(b * blk_rows, blk_rows)
                act(pltpu.make_async_copy(zeros_sc, xb_hbm.at[pl.ds(off, blk_rows), :], zsem))
                return carry
            lax.fori_loop(nu_ref[0], n_blk, per_block, 0)

        pad_rows(lambda cp: cp.start())
        tail_blocks(lambda cp: cp.start())
        pad_rows(lambda cp: cp.wait())
        tail_blocks(lambda cp: cp.wait())

    def issue(r, carry):
        for k in range(TOP_K):
            _row_copy(hp_ref, r, xb_hbm, dest_ref[0, 0, r * TOP_K + k], sem).start(priority=k % 2)
        return carry

    lax.fori_loop(0, bm, issue, 0, unroll=4)
    for k in range(TOP_K):
        pltpu.make_async_copy(hp_ref, xb_hbm.at[pl.ds(0, bm * SUBLANES), :], sem).wait()


def _dispatch(pad_lo, pad_hi, n_used, dest, hp, n_rows):
    t = dest.shape[0]
    bm = 512
    nt = t // bm
    n_blk = n_rows // MOE_BLOCK
    return pl.pallas_call(
        functools.partial(_dispatch_kernel, bm=bm, n_blk=n_blk),
        grid_spec=pltpu.PrefetchScalarGridSpec(
            num_scalar_prefetch=3,
            grid=(nt,),
            in_specs=[pl.BlockSpec((1, 1, bm * TOP_K), lambda i, *_: (i, 0, 0), memory_space=pltpu.SMEM),
                      pl.BlockSpec((bm * SUBLANES, LANES), lambda i, *_: (i, 0))],
            out_specs=pl.BlockSpec(memory_space=pl.ANY),
            scratch_shapes=[pltpu.VMEM((MOE_BLOCK * SUBLANES, LANES), U32),
                            pltpu.SemaphoreType.DMA(()), pltpu.SemaphoreType.DMA(())]),
        out_shape=jax.ShapeDtypeStruct((n_rows * SUBLANES, LANES), U32),
        compiler_params=_params("arbitrary"),
        name="moe_dispatch",
    )(pad_lo, pad_hi, n_used, dest.reshape(nt, 1, bm * TOP_K), hp)


def _expert_kernel(be_ref, nu_ref, bv_ref, xb_ref, wgu_ref, bgu_ref, wdn_ref, bdn_ref, o_ref):
    valid = bv_ref[pl.program_id(0)]

    def ffn(rows):
        los, his = [], []
        for c in range(SUBLANES):
            lo, hi = _unpack_pair(_load_row_tile_col(xb_ref, c, rows))
            los.append(lo.astype(BF16))
            his.append(hi.astype(BF16))
        lo = jnp.concatenate(los, axis=1)
        hi = jnp.concatenate(his, axis=1)
        half = lo.shape[1]
        gu = (jnp.dot(lo, wgu_ref[:half, :], preferred_element_type=F32)
              + jnp.dot(hi, wgu_ref[half:, :], preferred_element_type=F32) + bgu_ref[...])
        g = jnp.minimum(gu[:, :D_EXPERT], SWIGLU_LIMIT)
        u = jnp.clip(gu[:, D_EXPERT:], -SWIGLU_LIMIT, SWIGLU_LIMIT)
        act = (u + 1.0) * (g * jax.nn.sigmoid(SWIGLU_ALPHA * g))
        out = jnp.dot(act.astype(BF16), wdn_ref[...], preferred_element_type=F32) + bdn_ref[...]
        _store_row_tiles(o_ref, _pack_pair(out[:, :half], out[:, half:]))
        if rows < MOE_BLOCK:
            o_ref[rows * SUBLANES:, :] = jnp.zeros(((MOE_BLOCK - rows) * SUBLANES, LANES), o_ref.dtype)

    @pl.when(valid > MOE_BLOCK // 2)
    def _():
        ffn(MOE_BLOCK)

    @pl.when(jnp.logical_and(valid > 0, valid <= MOE_BLOCK // 2))
    def _():
        ffn(MOE_BLOCK // 2)

    @pl.when(valid == 0)
    def _():
        o_ref[...] = jnp.zeros(o_ref.shape, o_ref.dtype)


def _experts(block_e, n_used, block_valid, xb, w_gu, b_gu, w_dn, b_dn):
    n_blk = xb.shape[0] // (MOE_BLOCK * SUBLANES)
    ne, d, _ = w_gu.shape
    blk = pl.BlockSpec((MOE_BLOCK * SUBLANES, LANES), lambda b, be, nu, bv: (b, 0))
    blk_in = pl.BlockSpec((MOE_BLOCK * SUBLANES, LANES), lambda b, be, nu, bv: (jnp.minimum(b, nu[0] - 1), 0))
    return pl.pallas_call(
        _expert_kernel,
        grid_spec=pltpu.PrefetchScalarGridSpec(
            num_scalar_prefetch=3,
            grid=(n_blk,),
            in_specs=[blk_in,
                      pl.BlockSpec((None, d, 2 * D_EXPERT), lambda b, be, nu, bv: (be[b], 0, 0)),
                      pl.BlockSpec((None, 1, 2 * D_EXPERT), lambda b, be, nu, bv: (be[b], 0, 0)),
                      pl.BlockSpec((None, D_EXPERT, d), lambda b, be, nu, bv: (be[b], 0, 0)),
                      pl.BlockSpec((None, 1, d), lambda b, be, nu, bv: (be[b], 0, 0))],
            out_specs=blk),
        out_shape=jax.ShapeDtypeStruct(xb.shape, U32),
        compiler_params=_params("arbitrary"),
        name="moe_experts",
    )(block_e, n_used, block_valid, xb, w_gu, b_gu.reshape(ne, 1, -1), w_dn, b_dn.reshape(ne, 1, -1))


def _combine_kernel(dest_ref, dnext_ref, tw_ref, x_ref, g2_ref, ng_ref, sc_ref, sh_ref, outp_hbm, *rest,
                    bm, final):
    if final:
        ho_ref, buf, sem = rest
    else:
        xo_ref, ho_ref, buf, sem = rest
    i = pl.program_id(0)
    slot = i % 2

    def gather_tile(d_ref, s):
        def issue(r, carry):
            for k in range(TOP_K):
                _row_copy(outp_hbm, d_ref[0, 0, r * TOP_K + k], buf.at[s, k], r, sem.at[s]).start(priority=k % 2)
            return carry

        lax.fori_loop(0, bm, issue, 0, unroll=4)

    @pl.when(i == 0)
    def _():
        gather_tile(dest_ref, 0)

    @pl.when(i + 1 < pl.num_programs(0))
    def _():
        gather_tile(dnext_ref, 1 - slot)

    for k in range(TOP_K):
        pltpu.make_async_copy(outp_hbm.at[pl.ds(0, bm * SUBLANES), :], buf.at[slot, k], sem.at[slot]).wait()

    tw = tw_ref[...]
    wk = [tw[:, k:k + 1] for k in range(TOP_K)]
    y_lo, y_hi = [], []
    for c in range(SUBLANES):
        a_lo = a_hi = None
        for k in range(TOP_K):
            lo, hi = _unpack_pair(buf[slot, k, pl.ds(c, bm, stride=SUBLANES), :])
            a_lo = wk[k] * lo if a_lo is None else a_lo + wk[k] * lo
            a_hi = wk[k] * hi if a_hi is None else a_hi + wk[k] * hi
        y_lo.append(a_lo)
        y_hi.append(a_hi)
    x_new = x_ref[...] + g2_ref[...] * jnp.concatenate(y_lo + y_hi, axis=1)
    if final:
        ho_ref[...] = x_new * lax.rsqrt(jnp.mean(x_new * x_new, axis=-1, keepdims=True) + EPS) * ng_ref[...]
    else:
        xo_ref[...] = x_new
        ho_ref[...] = _rms_mod(x_new, ng_ref[...], sc_ref[...], sh_ref[...]).astype(BF16)


def _combine(dest, tw, x, g2, ng, sc, sh, outp, seq, final):
    t, d = x.shape
    bm = 256
    tps = seq // bm
    nt = t // bm
    row = lambda w: pl.BlockSpec((bm, w), lambda i: (i, 0))
    vec = pl.BlockSpec((None, 1, d), lambda i: (i // tps, 0, 0))
    dest3 = dest.reshape(nt, 1, bm * TOP_K)
    if final:
        out_specs = [row(d)]
        out_shape = [jax.ShapeDtypeStruct((t, d), F32)]
    else:
        out_specs = [row(d), row(d)]
        out_shape = [jax.ShapeDtypeStruct((t, d), F32), jax.ShapeDtypeStruct((t, d), BF16)]
    return pl.pallas_call(
        functools.partial(_combine_kernel, bm=bm, final=final),
        grid=(nt,),
        in_specs=[pl.BlockSpec((1, 1, bm * TOP_K), lambda i: (i, 0, 0), memory_space=pltpu.SMEM),
                  pl.BlockSpec((1, 1, bm * TOP_K), lambda i: (jnp.minimum(i + 1, nt - 1), 0, 0),
                               memory_space=pltpu.SMEM),
                  row(LANES), row(d), vec, pl.BlockSpec((1, d), lambda i: (0, 0)), vec, vec,
                  pl.BlockSpec(memory_space=pl.ANY)],
        out_specs=out_specs,
        out_shape=out_shape,
        scratch_shapes=[pltpu.VMEM((2, TOP_K, bm * SUBLANES, LANES), U32), pltpu.SemaphoreType.DMA((2,))],
        compiler_params=_params("arbitrary"),
        name="moe_combine",
    )(dest3, dest3, tw, x, g2, ng, sc, sh, outp)


def _routing_tables(top_i, rank, counts_f):
    t = top_i.shape[0]
    n_asg = t * TOP_K
    counts = counts_f.astype(I32)
    padded = ((counts + MOE_BLOCK - 1) // MOE_BLOCK) * MOE_BLOCK
    pend = jnp.cumsum(padded)
    pstart = pend - padded
    experts = jnp.arange(N_EXPERTS, dtype=I32)
    start_of = jnp.sum(jnp.where(top_i[:, :, None] == experts, pstart, 0), axis=-1)
    dest = (start_of + rank).astype(I32)
    n_rows = ((n_asg + N_EXPERTS * MOE_BLOCK + MOE_BLOCK - 1) // MOE_BLOCK) * MOE_BLOCK
    n_blk = n_rows // MOE_BLOCK
    first_row = jnp.arange(n_blk, dtype=I32) * MOE_BLOCK
    block_e = jnp.minimum(jnp.sum((pend[None, :] <= first_row[:, None]).astype(I32), axis=1), N_EXPERTS - 1)
    n_used = (pend[-1:] // MOE_BLOCK).astype(I32)
    pad_lo = (pstart + counts).astype(I32)
    seg_end = jnp.sum(jnp.where(block_e[:, None] == experts, pad_lo, 0), axis=-1)
    block_valid = jnp.clip(seg_end - first_row, 0, MOE_BLOCK).astype(I32)
    return dest, block_e, n_used, block_valid, n_rows, pad_lo, pend.astype(I32)


def _rotary_tables(positions):
    inv_freq = ROPE_THETA ** (-jnp.arange(0, ROT_DIM, 2, dtype=F32) / ROT_DIM)
    ang = positions.astype(F32).reshape(-1, 1) * inv_freq
    cos, sin = jnp.cos(ang), jnp.sin(ang)
    t = ang.shape[0]
    pad = jnp.zeros((t, DA_DK - ROT_DIM), F32)
    c64 = jnp.concatenate([cos, cos, pad + 1.0], axis=1)
    s1_64 = jnp.concatenate([jnp.zeros_like(sin), sin, pad], axis=1)
    s2_64 = jnp.concatenate([-sin, jnp.zeros_like(sin), pad], axis=1)
    rep = LANES // DA_DK
    return jnp.tile(c64, (1, rep)), jnp.tile(s1_64, (1, rep)), jnp.tile(s2_64, (1, rep)), cos.T, sin.T


def kernel(x, c, positions, w_ada, b_ada, norm_mix, norm_ffn, w_in, diff_lambda, diff_subln, gmlp_ln_g, gmlp_ln_b, gmlp_w_spatial, gmlp_b_spatial, pool_w, pool_scale, conv_w, w_branch, w_gate, b_gate, w_out, router_w, router_b, expert_w_gu, expert_b_gu, expert_w_down, expert_b_down, final_norm):
    bsz, seq, d = x.shape
    depth = w_ada.shape[0]
    t = bsz * seq
    assert seq % 512 == 0 and d == 2 * MIX_W and d // 2 == SUBLANES * LANES

    rot_c, rot_s1, rot_s2, cos_t, sin_t = _rotary_tables(positions)
    c_pad = jnp.zeros((SUBLANES, d), F32).at[:bsz].set(c)
    mod = _adaln(c_pad, w_ada, b_ada)[:, :bsz].reshape(depth, bsz, 6, 1, d)
    rw_pad = jnp.zeros((depth, d, LANES), F32).at[:, :, :N_EXPERTS].set(router_w)
    rb_pad = jnp.full((depth, 1, LANES), NEG_INF, F32).at[:, 0, :N_EXPERTS].set(router_b)

    w_in_b, w_gate_b, w_branch_b, w_out_b = (w.astype(BF16) for w in (w_in, w_gate, w_branch, w_out))

    xf = x.reshape(t, d)
    h = None
    for l in range(depth):
        lam_init = 0.8 - 0.6 * math.exp(-0.3 * l)
        sh1, sc1, g1, sh2, sc2, g2 = (mod[l, :, i] for i in range(6))
        if l == 0:
            h = _norm_mod(xf, norm_mix[l].reshape(1, d), sc1, sh1, seq)
        z, qt, vt = _inproj(h, w_in_b, l, rot_c, rot_s1, rot_s2, cos_t, sin_t)
        y_a, w_gu_b, w_dn_b = _attention(z, qt, vt, diff_lambda[l], diff_subln[l].reshape(DA_DV, 1),
                                         expert_w_gu, expert_w_down, l, bsz, seq, lam_init)
        bs_b = jnp.broadcast_to(gmlp_b_spatial[l][:, :, None], (GM_GROUPS, GM_CHUNK, GM_CHUNK))
        y_b = _gmlp(z, gmlp_ln_g[l].reshape(1, MIX_W), gmlp_ln_b[l].reshape(1, MIX_W),
                    gmlp_w_spatial[l], bs_b)
        y_c = _pool(z, pool_w[l].astype(BF16), pool_scale[l].reshape(1, MIX_W), seq)
        y_d = _conv(z, conv_w[l], seq)
        merged = _merge(h, (y_a, y_b, y_c, y_d), w_gate_b, b_gate[l], w_branch_b, l)
        xf, hp, ti, tw, tr, cnt = _outproj(merged, w_out_b, l, xf, g1, norm_ffn[l].reshape(1, d),
                                           sc2, sh2, rw_pad[l], rb_pad[l], seq)
        dest, block_e, n_used, block_valid, n_rows, pad_lo, pad_hi = _routing_tables(
            ti[:, :TOP_K], tr[:, :TOP_K], cnt[0, :N_EXPERTS])
        xb = _dispatch(pad_lo, pad_hi, n_used, dest, hp, n_rows)
        outp = _experts(block_e, n_used, block_valid, xb, w_gu_b, expert_b_gu[l], w_dn_b, expert_b_down[l])
        if l + 1 < depth:
            nsh1, nsc1 = mod[l + 1, :, 0], mod[l + 1, :, 1]
            xf, h = _combine(dest, tw, xf, g2, norm_mix[l + 1].reshape(1, d), nsc1, nsh1, outp, seq, False)
        else:
            zero = jnp.zeros((bsz, 1, d), F32)
            (out,) = _combine(dest, tw, xf, g2, final_norm.reshape(1, d), zero, zero, outp, seq, True)
    return out.reshape(bsz, seq, d)
```

```python
import functools
import math

import jax
import jax.numpy as jnp
from jax import lax
from jax.experimental import pallas as pl
from jax.experimental.pallas import tpu as pltpu

F32 = jnp.float32
BF16 = jnp.bfloat16
U32 = jnp.uint32
I32 = jnp.int32

MIX_W = 1024
DA_HEADS = 8
DA_DK = 64
DA_DV = 128
ROT_DIM = 16
ROPE_THETA = 500000.0
GM_GROUPS = 8
GM_CHUNK = 128
POOL_WINDOWS = (2, 4, 8, 16)
POOL_GC = 256
N_BRANCH = 4
N_EXPERTS = 32
TOP_K = 4
D_EXPERT = 1024
SWIGLU_LIMIT = 7.0
SWIGLU_ALPHA = 1.702
MOE_BLOCK = 512
EPS = 1e-6
NEG_INF = -1e30

LANES = 128
SUBLANES = 8
HALO = 16
ONES_ROWS = 16
VMEM_LIMIT = 56 * 1024 * 1024

COL_K, COL_GU, COL_GV, COL_POOL, COL_CB, COL_CC, COL_CH = range(7)
IN_Q, IN_K, IN_V = 0, 1, 2


def _params(*sem):
    return pltpu.CompilerParams(dimension_semantics=sem, vmem_limit_bytes=VMEM_LIMIT)


def _rms_mod(x, g, sc, sh):
    y = x * lax.rsqrt(jnp.mean(x * x, axis=-1, keepdims=True) + EPS) * g
    return y * (1.0 + sc) + sh


def _pack_pair(lo, hi):
    lo_bits = lax.bitcast_convert_type(lo.astype(BF16).astype(F32), U32) >> 16
    hi_bits = lax.bitcast_convert_type(hi.astype(BF16).astype(F32), U32) & jnp.uint32(0xFFFF0000)
    return hi_bits | lo_bits


def _unpack_pair(w):
    lo = lax.bitcast_convert_type(w << 16, F32)
    hi = lax.bitcast_convert_type(w & jnp.uint32(0xFFFF0000), F32)
    return lo, hi


def _store_row_tiles(ref, packed):
    n = packed.shape[0]
    for c in range(SUBLANES):
        ref[pl.ds(c, n, stride=SUBLANES), :] = packed[:, c * LANES:(c + 1) * LANES]


def _load_row_tile_col(ref, c, n):
    return ref[pl.ds(c, n, stride=SUBLANES), :]


def _adaln_kernel(c_ref, w_ref, b_ref, o_ref):
    c = c_ref[...]
    o_ref[...] = jnp.dot(c * jax.nn.sigmoid(c), w_ref[...], preferred_element_type=F32) + b_ref[...]


def _adaln(c_pad, w_ada, b_ada):
    depth, d, n = w_ada.shape
    bn = 1536
    rows = c_pad.shape[0]
    return pl.pallas_call(
        _adaln_kernel,
        grid=(depth, n // bn),
        in_specs=[pl.BlockSpec((rows, d), lambda l, j: (0, 0)),
                  pl.BlockSpec((None, d, bn), lambda l, j: (l, 0, j)),
                  pl.BlockSpec((None, 1, bn), lambda l, j: (l, 0, j))],
        out_specs=pl.BlockSpec((None, rows, bn), lambda l, j: (l, 0, j)),
        out_shape=jax.ShapeDtypeStruct((depth, rows, n), F32),
        compiler_params=_params("arbitrary", "arbitrary"),
        name="adaln",
    )(c_pad, w_ada, b_ada.reshape(depth, 1, n))


def _norm_kernel(x_ref, g_ref, sc_ref, sh_ref, h_ref):
    h_ref[...] = _rms_mod(x_ref[...], g_ref[...], sc_ref[...], sh_ref[...]).astype(BF16)


def _norm_mod(x, g, sc, sh, seq):
    t, d = x.shape
    bm = 512
    tps = seq // bm
    vec = pl.BlockSpec((None, 1, d), lambda i: (i // tps, 0, 0))
    return pl.pallas_call(
        _norm_kernel,
        grid=(t // bm,),
        in_specs=[pl.BlockSpec((bm, d), lambda i: (i, 0)),
                  pl.BlockSpec((1, d), lambda i: (0, 0)), vec, vec],
        out_specs=pl.BlockSpec((bm, d), lambda i: (i, 0)),
        out_shape=jax.ShapeDtypeStruct((t, d), BF16),
        compiler_params=_params("arbitrary"),
        name="norm_mod",
    )(x, g, sc, sh)


def _inproj_kernel(h_ref, w_ref, c_ref, s1_ref, s2_ref, ct_ref, st_ref, z_ref, qt_ref, vt_ref):
    j = pl.program_id(1)
    acc = jnp.dot(h_ref[...], w_ref[...], preferred_element_type=F32)
    n_tiles = acc.shape[1] // LANES
    half = ROT_DIM // 2
    z_ref[...] = acc.astype(BF16)

    @pl.when(j == IN_Q)
    def _():
        scale = DA_DK ** -0.5 * math.log2(math.e)
        cos = ct_ref[...] * scale
        sin = st_ref[...] * scale
        for t in range(n_tiles):
            xt = acc[:, t * LANES:(t + 1) * LANES].T
            parts = []
            for comp in range(LANES // DA_DK):
                base = comp * DA_DK
                x1 = xt[base:base + half]
                x2 = xt[base + half:base + ROT_DIM]
                parts += [x1 * cos - x2 * sin, x2 * cos + x1 * sin, xt[base + ROT_DIM:base + DA_DK] * scale]
            qt_ref[t * LANES:(t + 1) * LANES, :] = jnp.concatenate(parts, axis=0).astype(BF16)

    @pl.when(j == IN_K)
    def _():
        for t in range(n_tiles):
            xt = acc[:, t * LANES:(t + 1) * LANES]
            r = (xt * c_ref[...] + pltpu.roll(xt, half, 1) * s1_ref[...]
                 + pltpu.roll(xt, LANES - half, 1) * s2_ref[...])
            z_ref[:, t * LANES:(t + 1) * LANES] = r.astype(BF16)

    @pl.when(j == IN_V)
    def _():
        for t in range(n_tiles):
            vt_ref[t * LANES:(t + 1) * LANES, :] = acc[:, t * LANES:(t + 1) * LANES].T.astype(BF16)


def _inproj(h, w_in, layer, rot_c, rot_s1, rot_s2, cos_t, sin_t):
    t, d = h.shape
    n = w_in.shape[2]
    bm, bn = 1024, MIX_W
    nz = n // bn - 2
    rot = pl.BlockSpec((bm, LANES), lambda i, j: (i, 0))
    rot_t = pl.BlockSpec((ROT_DIM // 2, bm), lambda i, j: (0, i))
    z_col = lambda j: jnp.where(j <= IN_K, 0, jnp.maximum(j - 2, 1))
    return pl.pallas_call(
        _inproj_kernel,
        grid=(t // bm, n // bn),
        in_specs=[pl.BlockSpec((bm, d), lambda i, j: (i, 0)),
                  pl.BlockSpec((None, d, bn), lambda i, j: (layer, 0, j)), rot, rot, rot, rot_t, rot_t],
        out_specs=[pl.BlockSpec((bm, bn), lambda i, j: (i, z_col(j))),
                   pl.BlockSpec((bn, bm), lambda i, j: (0, i)),
                   pl.BlockSpec((bn, bm), lambda i, j: (0, i))],
        out_shape=[jax.ShapeDtypeStruct((t, nz * bn), BF16),
                   jax.ShapeDtypeStruct((bn, t), BF16),
                   jax.ShapeDtypeStruct((bn, t), BF16)],
        compiler_params=_params("arbitrary", "arbitrary"),
        name="inproj",
    )(h, w_in, rot_c, rot_s1, rot_s2, cos_t, sin_t)


def _attn_kernel(qt_ref, k_ref, vt_ref, lam_ref, sub_ref, wgu_ref, wdn_ref, o_ref, wgu_o_ref, wdn_o_ref,
                 sa_ref, sb_ref, ma_sc, mb_sc, m_sc, acc_sc, *, tq, lam_init):
    i = pl.program_id(2)
    wgu_o_ref[...] = wgu_ref[...].astype(BF16)
    wdn_o_ref[...] = wdn_ref[...].astype(BF16)
    qt = qt_ref[...]
    row = lax.broadcasted_iota(I32, qt.shape, 0)
    zero = jnp.zeros_like(qt)
    qs = jnp.concatenate([jnp.where(row < DA_DK, qt, zero), jnp.where(row >= DA_DK, qt, zero)], axis=1)
    m_sc[...] = jnp.full(m_sc.shape, NEG_INF, F32)
    acc_sc[...] = jnp.zeros(acc_sc.shape, F32)

    def scores(j, s_ref, smax_ref):
        off = pl.multiple_of(j * tq, tq)
        s = jnp.dot(k_ref[pl.ds(off, tq), :], qs, preferred_element_type=F32)
        s_ref[...] = s
        smax_ref[...] = jnp.max(s, axis=0, keepdims=True)

    def softmax_pv(j, s_ref, smax_ref, masked):
        off = pl.multiple_of(j * tq, tq)
        s = s_ref[...]
        if masked:
            key = lax.broadcasted_iota(I32, s.shape, 0)
            c2 = lax.broadcasted_iota(I32, s.shape, 1)
            s = jnp.where(key <= jnp.where(c2 >= tq, c2 - tq, c2), s, NEG_INF)
            s_max = jnp.max(s, axis=0, keepdims=True)
        else:
            s_max = smax_ref[...]
        m_prev = m_sc[...]
        m_new = jnp.maximum(m_prev, s_max)
        alpha = jnp.exp2(m_prev - m_new)
        p = jnp.exp2(s - m_new).astype(BF16)
        vt_aug = jnp.concatenate([vt_ref[:, pl.ds(off, tq)], jnp.ones((ONES_ROWS, tq), BF16)], axis=0)
        acc_sc[...] = alpha * acc_sc[...] + jnp.dot(vt_aug, p, preferred_element_type=F32)
        m_sc[...] = m_new

    scores(0, sa_ref, ma_sc)

    def pair(j):
        scores(j + 1, sb_ref, mb_sc)
        softmax_pv(j, sa_ref, ma_sc, False)
        scores(j + 2, sa_ref, ma_sc)
        softmax_pv(j + 1, sb_ref, mb_sc, False)

    def octet(jj, carry):
        for q in range(4):
            pair(8 * jj + 2 * q)
        return carry

    n_oct = i // 8
    lax.fori_loop(0, n_oct, octet, 0)
    done = 8 * n_oct
    rest = i - done

    @pl.when(rest >= 4)
    def _():
        pair(done)
        pair(done + 2)

    done = done + jnp.where(rest >= 4, 4, 0)

    @pl.when(i - done >= 2)
    def _():
        pair(done)

    @pl.when(i % 2 == 1)
    def _():
        scores(i, sb_ref, mb_sc)
        softmax_pv(i - 1, sa_ref, ma_sc, False)
        softmax_pv(i, sb_ref, mb_sc, True)

    @pl.when(i % 2 == 0)
    def _():
        softmax_pv(i, sa_ref, ma_sc, True)

    lv = lam_ref[...]
    lam = (jnp.exp(jnp.sum(lv[0:1] * lv[1:2], axis=-1, keepdims=True))
           - jnp.exp(jnp.sum(lv[2:3] * lv[3:4], axis=-1, keepdims=True)) + lam_init)
    acc = acc_sc[:DA_DV, :]
    l = acc_sc[DA_DV:DA_DV + 1, :]
    o = acc[:, :tq] / l[:, :tq] - lam * (acc[:, tq:] / l[:, tq:])
    y = o * lax.rsqrt(jnp.mean(o * o, axis=0, keepdims=True) + EPS) * sub_ref[...] * (1.0 - lam_init)
    o_ref[...] = y.T.astype(o_ref.dtype)


def _attention(z, qt, vt, lam_vecs, subln_col, w_gu, w_dn, layer, bsz, seq, lam_init):
    t = z.shape[0]
    tq = min(512, seq)
    nq = seq // tq
    steps = bsz * DA_HEADS * nq
    _, ne, d, gu_w = w_gu.shape
    gu_rows, dn_rows = ne * d // steps, ne * D_EXPERT // steps
    assert gu_rows * steps == ne * d and dn_rows * steps == ne * D_EXPERT and dn_rows % HALO == 0
    step = lambda b, h, i: (b * DA_HEADS + h) * nq + i
    y_a, gu_b, dn_b = pl.pallas_call(
        functools.partial(_attn_kernel, tq=tq, lam_init=lam_init),
        grid=(bsz, DA_HEADS, nq),
        in_specs=[pl.BlockSpec((DA_DV, tq), lambda b, h, i: (h, b * nq + i)),
                  pl.BlockSpec((seq, DA_DV), lambda b, h, i: (b, COL_K * DA_HEADS + h)),
                  pl.BlockSpec((DA_DV, seq), lambda b, h, i: (h, b)),
                  pl.BlockSpec((4, DA_DK), lambda b, h, i: (0, 0)),
                  pl.BlockSpec((DA_DV, 1), lambda b, h, i: (0, 0)),
                  pl.BlockSpec((None, gu_rows, gu_w), lambda b, h, i: (layer, step(b, h, i), 0)),
                  pl.BlockSpec((None, dn_rows, d), lambda b, h, i: (layer, step(b, h, i), 0))],
        out_specs=[pl.BlockSpec((tq, DA_DV), lambda b, h, i: (b * nq + i, h)),
                   pl.BlockSpec((gu_rows, gu_w), lambda b, h, i: (step(b, h, i), 0)),
                   pl.BlockSpec((dn_rows, d), lambda b, h, i: (step(b, h, i), 0))],
        out_shape=[jax.ShapeDtypeStruct((t, MIX_W), BF16),
                   jax.ShapeDtypeStruct((ne * d, gu_w), BF16),
                   jax.ShapeDtypeStruct((ne * D_EXPERT, d), BF16)],
        scratch_shapes=[pltpu.VMEM((tq, 2 * tq), F32), pltpu.VMEM((tq, 2 * tq), F32),
                        pltpu.VMEM((1, 2 * tq), F32), pltpu.VMEM((1, 2 * tq), F32),
                        pltpu.VMEM((1, 2 * tq), F32), pltpu.VMEM((DA_DV + ONES_ROWS, 2 * tq), F32)],
        compiler_params=_params("arbitrary", "arbitrary", "arbitrary"),
        name="diff_attn",
    )(qt, z, vt, lam_vecs, subln_col, w_gu.reshape(w_gu.shape[0], ne * d, gu_w),
      w_dn.reshape(w_dn.shape[0], ne * D_EXPERT, d))
    return y_a, gu_b.reshape(ne, d, gu_w), dn_b.reshape(ne, D_EXPERT, d)


def _gelu(x):
    return 0.5 * x * (1.0 + lax.erf(x * math.sqrt(0.5)))


def _gmlp_kernel(u_ref, v_ref, g_ref, b_ref, ws_ref, bs_ref, o_ref, *, bm):
    v = _gelu(v_ref[...].astype(F32))
    mu = jnp.mean(v, axis=-1, keepdims=True)
    vc = v - mu
    var = jnp.mean(vc * vc, axis=-1, keepdims=True)
    vn = (vc * lax.rsqrt(var + EPS) * g_ref[...] + b_ref[...]).astype(BF16)
    r = lax.broadcasted_iota(I32, (GM_CHUNK, GM_CHUNK), 0)
    c = lax.broadcasted_iota(I32, (GM_CHUNK, GM_CHUNK), 1)
    for g in range(GM_GROUPS):
        cols = slice(g * GM_CHUNK, (g + 1) * GM_CHUNK)
        wm = jnp.where(c <= r, ws_ref[g], 0.0).astype(BF16)
        bias = bs_ref[g]
        for n in range(bm // GM_CHUNK):
            rows = slice(n * GM_CHUNK, (n + 1) * GM_CHUNK)
            sv = jnp.dot(wm, vn[rows, cols], preferred_element_type=F32) + bias
            u = _gelu(u_ref[rows, cols].astype(F32))
            o_ref[rows, cols] = (u * sv).astype(o_ref.dtype)


def _gmlp(z, ln_g, ln_b, ws, bs_b):
    t = z.shape[0]
    bm = 512
    full = lambda shape: pl.BlockSpec(shape, lambda i: (0,) * len(shape))
    return pl.pallas_call(
        functools.partial(_gmlp_kernel, bm=bm),
        grid=(t // bm,),
        in_specs=[pl.BlockSpec((bm, MIX_W), lambda i: (i, COL_GU)),
                  pl.BlockSpec((bm, MIX_W), lambda i: (i, COL_GV)),
                  full((1, MIX_W)), full((1, MIX_W)),
                  full((GM_GROUPS, GM_CHUNK, GM_CHUNK)), full((GM_GROUPS, GM_CHUNK, GM_CHUNK))],
        out_specs=pl.BlockSpec((bm, MIX_W), lambda i: (i, 0)),
        out_shape=jax.ShapeDtypeStruct((t, MIX_W), BF16),
        compiler_params=_params("arbitrary"),
        name="gmlp",
    )(z, z, ln_g, ln_b, ws, bs_b)


def _pool_kernel(z_ref, halo_ref, pw_ref, ps_ref, o_ref, band_sc, bandh_sc, *, bm, tps):
    p0 = (pl.program_id(0) % tps) * bm

    @pl.when(pl.program_id(0) == 0)
    def _():
        d = (lax.broadcasted_iota(I32, (bm, bm), 0) - lax.broadcasted_iota(I32, (bm, bm), 1))
        dh = (lax.broadcasted_iota(I32, (bm, HALO), 0) + HALO - lax.broadcasted_iota(I32, (bm, HALO), 1))
        for g, w in enumerate(POOL_WINDOWS):
            band_sc[g] = jnp.where(d >= 0, jnp.where(d < w, 1.0, 0.0), 0.0).astype(BF16)
            bandh_sc[g] = jnp.where(dh < w, 1.0, 0.0).astype(BF16)

    pos = p0 + lax.broadcasted_iota(I32, (bm, 1), 0)
    for g, w in enumerate(POOL_WINDOWS):
        cols = slice(g * POOL_GC, (g + 1) * POOL_GC)
        x = z_ref[:, cols]
        hx = halo_ref[:, cols]
        hx = jnp.where(p0 > 0, hx, jnp.zeros_like(hx))
        win = (jnp.dot(band_sc[g], x, preferred_element_type=F32)
               + jnp.dot(bandh_sc[g], hx, preferred_element_type=F32))
        cnt = jnp.minimum(pos + 1, w).astype(F32)
        pooled = (win / cnt - x.astype(F32)).astype(BF16)
        mixed = jnp.dot(pooled, pw_ref[g], preferred_element_type=F32)
        o_ref[:, cols] = (mixed * ps_ref[:, cols]).astype(o_ref.dtype)


def _halo_spec(bm, col):
    return pl.BlockSpec((HALO, MIX_W), lambda i: (jnp.maximum(i * (bm // HALO) - 1, 0), col))


def _pool(z, pw, ps, seq):
    t = z.shape[0]
    bm = min(512, seq)
    ng = len(POOL_WINDOWS)
    return pl.pallas_call(
        functools.partial(_pool_kernel, bm=bm, tps=seq // bm),
        grid=(t // bm,),
        in_specs=[pl.BlockSpec((bm, MIX_W), lambda i: (i, COL_POOL)),
                  _halo_spec(bm, COL_POOL),
                  pl.BlockSpec((ng, POOL_GC, POOL_GC), lambda i: (0, 0, 0)),
                  pl.BlockSpec((1, MIX_W), lambda i: (0, 0))],
        out_specs=pl.BlockSpec((bm, MIX_W), lambda i: (i, 0)),
        out_shape=jax.ShapeDtypeStruct((t, MIX_W), BF16),
        scratch_shapes=[pltpu.VMEM((ng, bm, bm), BF16), pltpu.VMEM((ng, bm, HALO), BF16)],
        compiler_params=_params("arbitrary"),
        name="pool",
    )(z, z, pw, ps)


def _conv_kernel(b_ref, c_ref, h_ref, hc_ref, hh_ref, cw_ref, o_ref, *, tps):
    first = (pl.program_id(0) % tps) == 0
    xin = c_ref[...].astype(F32) * h_ref[...].astype(F32)
    hal = hc_ref[...].astype(F32) * hh_ref[...].astype(F32)
    hal = jnp.where(first, 0.0, hal)
    hm1 = hal[HALO - 1:HALO, :]
    hm2 = hal[HALO - 2:HALO - 1, :]
    row = lax.broadcasted_iota(I32, xin.shape, 0)
    s1 = jnp.where(row == 0, hm1, pltpu.roll(xin, 1, 0))
    s2 = jnp.where(row == 0, hm2, jnp.where(row == 1, hm1, pltpu.roll(xin, 2, 0)))
    cw = cw_ref[...]
    y = cw[0:1] * s2 + cw[1:2] * s1 + cw[2:3] * xin
    o_ref[...] = (b_ref[...].astype(F32) * y).astype(o_ref.dtype)


def _conv(z, cw, seq):
    t = z.shape[0]
    bm = min(512, seq)
    blk = lambda col: pl.BlockSpec((bm, MIX_W), lambda i: (i, col))
    return pl.pallas_call(
        functools.partial(_conv_kernel, tps=seq // bm),
        grid=(t // bm,),
        in_specs=[blk(COL_CB), blk(COL_CC), blk(COL_CH), _halo_spec(bm, COL_CC), _halo_spec(bm, COL_CH),
                  pl.BlockSpec(cw.shape, lambda i: (0, 0))],
        out_specs=pl.BlockSpec((bm, MIX_W), lambda i: (i, 0)),
        out_shape=jax.ShapeDtypeStruct((t, MIX_W), BF16),
        compiler_params=_params("arbitrary"),
        name="short_conv",
    )(z, z, z, z, z, cw)


def _merge_kernel(h_ref, ya_ref, yb_ref, yc_ref, yd_ref, wg_ref, bg_ref, wb_ref, o_ref):
    h = h_ref[...]
    bg = bg_ref[...]
    merged = None
    for i, y_ref in enumerate((ya_ref, yb_ref, yc_ref, yd_ref)):
        gate = jax.nn.sigmoid(jnp.dot(h, wg_ref[i], preferred_element_type=F32) + bg[i:i + 1])
        term = gate * jnp.dot(y_ref[...], wb_ref[i], preferred_element_type=F32)
        merged = term if merged is None else merged + term
    o_ref[...] = merged.astype(o_ref.dtype)


def _merge(h, ys, wg, bg, wb, layer):
    t, d = h.shape
    bm, bn = 1024, 256
    yspec = pl.BlockSpec((bm, MIX_W), lambda i, j: (i, 0))
    return pl.pallas_call(
        _merge_kernel,
        grid=(t // bm, d // bn),
        in_specs=[pl.BlockSpec((bm, d), lambda i, j: (i, 0)), yspec, yspec, yspec, yspec,
                  pl.BlockSpec((None, N_BRANCH, d, bn), lambda i, j: (layer, 0, 0, j)),
                  pl.BlockSpec((N_BRANCH, bn), lambda i, j: (0, j)),
                  pl.BlockSpec((None, N_BRANCH, MIX_W, bn), lambda i, j: (layer, 0, 0, j))],
        out_specs=pl.BlockSpec((bm, bn), lambda i, j: (i, j)),
        out_shape=jax.ShapeDtypeStruct((t, d), BF16),
        compiler_params=_params("arbitrary", "arbitrary"),
        name="branch_merge",
    )(h, *ys, wg, bg, wb)


def _outproj_kernel(m_ref, wo_ref, x_ref, g1_ref, ng_ref, sc_ref, sh_ref, rw_ref, rb_ref,
                    xo_ref, hp_ref, ti_ref, tw_ref, tr_ref, cnt_ref, carry_sc):
    out = jnp.dot(m_ref[...], wo_ref[...], preferred_element_type=F32)
    x_new = x_ref[...] + g1_ref[...] * out
    xo_ref[...] = x_new
    h2 = _rms_mod(x_new, ng_ref[...], sc_ref[...], sh_ref[...])
    half = h2.shape[1] // 2
    _store_row_tiles(hp_ref, _pack_pair(h2[:, :half], h2[:, half:]))

    logits = jnp.dot(h2, rw_ref[...], preferred_element_type=F32) + rb_ref[...]
    lane = lax.broadcasted_iota(I32, logits.shape, 1).astype(F32)
    vals, idxs = [], []
    cur = logits
    for _ in range(TOP_K):
        m = jnp.max(cur, axis=-1, keepdims=True)
        idx = jnp.min(jnp.where(cur == m, lane, float(LANES)), axis=-1, keepdims=True)
        vals.append(m)
        idxs.append(idx)
        cur = jnp.where(lane == idx, -3e38, cur)
    es = [jnp.exp(v - vals[0]) for v in vals]
    den = es[0] + es[1] + es[2] + es[3]

    @pl.when(pl.program_id(0) == 0)
    def _():
        carry_sc[...] = jnp.zeros(carry_sc.shape, F32)

    onehot = jnp.zeros(logits.shape, F32)
    for k in range(TOP_K):
        onehot = onehot + jnp.where(lane == idxs[k], 1.0, 0.0)
    bm = logits.shape[0]
    tri = jnp.where(lax.broadcasted_iota(I32, (bm, bm), 0) > lax.broadcasted_iota(I32, (bm, bm), 1),
                    1.0, 0.0).astype(BF16)
    before = jnp.dot(tri, onehot.astype(BF16), preferred_element_type=F32) + carry_sc[...]
    carry = carry_sc[...] + jnp.sum(onehot, axis=0, keepdims=True)
    carry_sc[...] = carry
    cnt_ref[...] = jnp.broadcast_to(carry, cnt_ref.shape)

    ti = jnp.zeros(logits.shape, F32)
    tw = jnp.zeros(logits.shape, F32)
    tr = jnp.zeros(logits.shape, F32)
    for k in range(TOP_K):
        rank = jnp.sum(jnp.where(lane == idxs[k], before, 0.0), axis=-1, keepdims=True)
        ti = jnp.where(lane == float(k), idxs[k], ti)
        tw = jnp.where(lane == float(k), es[k] / den, tw)
        tr = jnp.where(lane == float(k), rank, tr)
    ti_ref[...] = ti.astype(I32)
    tw_ref[...] = tw
    tr_ref[...] = tr.astype(I32)


def _outproj(merged, wo, layer, x, g1, ng, sc, sh, rw, rb, seq):
    t, d = x.shape
    bm = 512
    tps = seq // bm
    row = lambda w: pl.BlockSpec((bm, w), lambda i: (i, 0))
    vec = pl.BlockSpec((None, 1, d), lambda i: (i // tps, 0, 0))
    full = lambda a: pl.BlockSpec(a.shape, lambda i: (0, 0))
    return pl.pallas_call(
        _outproj_kernel,
        grid=(t // bm,),
        in_specs=[row(d), pl.BlockSpec((None, d, d), lambda i: (layer, 0, 0)), row(d), vec, full(ng), vec, vec,
                  full(rw), full(rb)],
        out_specs=[row(d), pl.BlockSpec((bm * SUBLANES, LANES), lambda i: (i, 0)),
                   row(LANES), row(LANES), row(LANES),
                   pl.BlockSpec((SUBLANES, LANES), lambda i: (0, 0))],
        out_shape=[jax.ShapeDtypeStruct((t, d), F32), jax.ShapeDtypeStruct((t * SUBLANES, LANES), U32),
                   jax.ShapeDtypeStruct((t, LANES), I32), jax.ShapeDtypeStruct((t, LANES), F32),
                   jax.ShapeDtypeStruct((t, LANES), I32), jax.ShapeDtypeStruct((SUBLANES, LANES), F32)],
        scratch_shapes=[pltpu.VMEM((1, LANES), F32)],
        compiler_params=_params("arbitrary"),
        name="outproj_router",
    )(merged, wo, x, g1, ng, sc, sh, rw, rb)


def _row_copy(src_ref, src_row, dst_ref, dst_row, sem):
    return pltpu.make_async_copy(src_ref.at[pl.ds(pl.multiple_of(src_row * SUBLANES, SUBLANES), SUBLANES), :],
                                 dst_ref.at[pl.ds(pl.multiple_of(dst_row * SUBLANES, SUBLANES), SUBLANES), :],
                                 sem)


def _dispatch_kernel(pad_lo_ref, pad_hi_ref, nu_ref, dest_ref, hp_ref, xb_hbm, zeros_sc, sem, zsem,
                     *, bm, n_blk):
    blk_rows = MOE_BLOCK * SUBLANES

    @pl.when(pl.program_id(0) == 0)
    def _():
        zeros_sc[...] = jnp.zeros(zeros_sc.shape, zeros_sc.dtype)

        def pad_rows(act):
            def per_expert(e, carry):
                row = pad_lo_ref[e]
                n_pad = pad_hi_ref[e] - row
                p = MOE_BLOCK // 2
                while p >= 1:
                    take = (n_pad & p) != 0

                    @pl.when(take)
                    def _(row=row, p=p):
                        dst = xb_hbm.at[pl.ds(pl.multiple_of(row * SUBLANES, SUBLANES), p * SUBLANES), :]
                        act(pltpu.make_async_copy(zeros_sc.at[pl.ds(0, p * SUBLANES), :], dst, zsem))

                    row = row + jnp.where(take, p, 0)
                    p //= 2
                return carry
            lax.fori_loop(0, N_EXPERTS, per_expert, 0)

        def tail_blocks(act):
            def per_block(b, carry):
                off = pl.multiple_of(b * blk_rows, blk_rows)
                act(pltpu.make_async_copy(zeros_sc, xb_hbm.at[pl.ds(off, blk_rows), :], zsem))
                return carry
            lax.fori_loop(nu_ref[0], n_blk, per_block, 0)

        pad_rows(lambda cp: cp.start())
        tail_blocks(lambda cp: cp.start())
        pad_rows(lambda cp: cp.wait())
        tail_blocks(lambda cp: cp.wait())

    def issue(r, carry):
        for k in range(TOP_K):
            _row_copy(hp_ref, r, xb_hbm, dest_ref[0, 0, r * TOP_K + k], sem).start(priority=k % 2)
        return carry

    lax.fori_loop(0, bm, issue, 0, unroll=4)
    for k in range(TOP_K):
        pltpu.make_async_copy(hp_ref, xb_hbm.at[pl.ds(0, bm * SUBLANES), :], sem).wait()


def _dispatch(pad_lo, pad_hi, n_used, dest, hp, n_rows):
    t = dest.shape[0]
    bm = 512
    nt = t // bm
    n_blk = n_rows // MOE_BLOCK
    return pl.pallas_call(
        functools.partial(_dispatch_kernel, bm=bm, n_blk=n_blk),
        grid_spec=pltpu.PrefetchScalarGridSpec(
            num_scalar_prefetch=3,
            grid=(nt,),
            in_specs=[pl.BlockSpec((1, 1, bm * TOP_K), lambda i, *_: (i, 0, 0), memory_space=pltpu.SMEM),
                      pl.BlockSpec((bm * SUBLANES, LANES), lambda i, *_: (i, 0))],
            out_specs=pl.BlockSpec(memory_space=pl.ANY),
            scratch_shapes=[pltpu.VMEM((MOE_BLOCK * SUBLANES, LANES), U32),
                            pltpu.SemaphoreType.DMA(()), pltpu.SemaphoreType.DMA(())]),
        out_shape=jax.ShapeDtypeStruct((n_rows * SUBLANES, LANES), U32),
        compiler_params=_params("arbitrary"),
        name="moe_dispatch",
    )(pad_lo, pad_hi, n_used, dest.reshape(nt, 1, bm * TOP_K), hp)


def _expert_kernel(be_ref, nu_ref, bv_ref, xb_ref, wgu_ref, bgu_ref, wdn_ref, bdn_ref, o_ref):
    valid = bv_ref[pl.program_id(0)]

    def ffn(rows):
        los, his = [], []
        for c in range(SUBLANES):
            lo, hi = _unpack_pair(_load_row_tile_col(xb_ref, c, rows))
            los.append(lo.astype(BF16))
            his.append(hi.astype(BF16))
        lo = jnp.concatenate(los, axis=1)
        hi = jnp.concatenate(his, axis=1)
        half = lo.shape[1]
        gu = (jnp.dot(lo, wgu_ref[:half, :], preferred_element_type=F32)
              + jnp.dot(hi, wgu_ref[half:, :], preferred_element_type=F32) + bgu_ref[...])
        g = jnp.minimum(gu[:, :D_EXPERT], SWIGLU_LIMIT)
        u = jnp.clip(gu[:, D_EXPERT:], -SWIGLU_LIMIT, SWIGLU_LIMIT)
        act = (u + 1.0) * (g * jax.nn.sigmoid(SWIGLU_ALPHA * g))
        out = jnp.dot(act.astype(BF16), wdn_ref[...], preferred_element_type=F32) + bdn_ref[...]
        _store_row_tiles(o_ref, _pack_pair(out[:, :half], out[:, half:]))
        if rows < MOE_BLOCK:
            o_ref[rows * SUBLANES:, :] = jnp.zeros(((MOE_BLOCK - rows) * SUBLANES, LANES), o_ref.dtype)

    @pl.when(valid > MOE_BLOCK // 2)
    def _():
        ffn(MOE_BLOCK)

    @pl.when(jnp.logical_and(valid > 0, valid <= MOE_BLOCK // 2))
    def _():
        ffn(MOE_BLOCK // 2)

    @pl.when(valid == 0)
    def _():
        o_ref[...] = jnp.zeros(o_ref.shape, o_ref.dtype)


def _experts(block_e, n_used, block_valid, xb, w_gu, b_gu, w_dn, b_dn):
    n_blk = xb.shape[0] // (MOE_BLOCK * SUBLANES)
    ne, d, _ = w_gu.shape
    blk = pl.BlockSpec((MOE_BLOCK * SUBLANES, LANES), lambda b, be, nu, bv: (b, 0))
    blk_in = pl.BlockSpec((MOE_BLOCK * SUBLANES, LANES), lambda b, be, nu, bv: (jnp.minimum(b, nu[0] - 1), 0))
    return pl.pallas_call(
        _expert_kernel,
        grid_spec=pltpu.PrefetchScalarGridSpec(
            num_scalar_prefetch=3,
            grid=(n_blk,),
            in_specs=[blk_in,
                      pl.BlockSpec((None, d, 2 * D_EXPERT), lambda b, be, nu, bv: (be[b], 0, 0)),
                      pl.BlockSpec((None, 1, 2 * D_EXPERT), lambda b, be, nu, bv: (be[b], 0, 0)),
                      pl.BlockSpec((None, D_EXPERT, d), lambda b, be, nu, bv: (be[b], 0, 0)),
                      pl.BlockSpec((None, 1, d), lambda b, be, nu, bv: (be[b], 0, 0))],
            out_specs=blk),
        out_shape=jax.ShapeDtypeStruct(xb.shape, U32),
        compiler_params=_params("arbitrary"),
        name="moe_experts",
    )(block_e, n_used, block_valid, xb, w_gu, b_gu.reshape(ne, 1, -1), w_dn, b_dn.reshape(ne, 1, -1))


def _combine_kernel(dest_ref, dnext_ref, tw_ref, x_ref, g2_ref, ng_ref, sc_ref, sh_ref, outp_hbm, *rest,
                    bm, final):
    if final:
        ho_ref, buf, sem = rest
    else:
        xo_ref, ho_ref, buf, sem = rest
    i = pl.program_id(0)
    slot = i % 2

    def gather_tile(d_ref, s):
        def issue(r, carry):
            for k in range(TOP_K):
                _row_copy(outp_hbm, d_ref[0, 0, r * TOP_K + k], buf.at[s, k], r, sem.at[s]).start(priority=k % 2)
            return carry

        lax.fori_loop(0, bm, issue, 0, unroll=4)

    @pl.when(i == 0)
    def _():
        gather_tile(dest_ref, 0)

    @pl.when(i + 1 < pl.num_programs(0))
    def _():
        gather_tile(dnext_ref, 1 - slot)

    for k in range(TOP_K):
        pltpu.make_async_copy(outp_hbm.at[pl.ds(0, bm * SUBLANES), :], buf.at[slot, k], sem.at[slot]).wait()

    tw = tw_ref[...]
    wk = [tw[:, k:k + 1] for k in range(TOP_K)]
    y_lo, y_hi = [], []
    for c in range(SUBLANES):
        a_lo = a_hi = None
        for k in range(TOP_K):
            lo, hi = _unpack_pair(buf[slot, k, pl.ds(c, bm, stride=SUBLANES), :])
            a_lo = wk[k] * lo if a_lo is None else a_lo + wk[k] * lo
            a_hi = wk[k] * hi if a_hi is None else a_hi + wk[k] * hi
        y_lo.append(a_lo)
        y_hi.append(a_hi)
    x_new = x_ref[...] + g2_ref[...] * jnp.concatenate(y_lo + y_hi, axis=1)
    if final:
        ho_ref[...] = x_new * lax.rsqrt(jnp.mean(x_new * x_new, axis=-1, keepdims=True) + EPS) * ng_ref[...]
    else:
        xo_ref[...] = x_new
        ho_ref[...] = _rms_mod(x_new, ng_ref[...], sc_ref[...], sh_ref[...]).astype(BF16)


def _combine(dest, tw, x, g2, ng, sc, sh, outp, seq, final):
    t, d = x.shape
    bm = 256
    tps = seq // bm
    nt = t // bm
    row = lambda w: pl.BlockSpec((bm, w), lambda i: (i, 0))
    vec = pl.BlockSpec((None, 1, d), lambda i: (i // tps, 0, 0))
    dest3 = dest.reshape(nt, 1, bm * TOP_K)
    if final:
        out_specs = [row(d)]
        out_shape = [jax.ShapeDtypeStruct((t, d), F32)]
    else:
        out_specs = [row(d), row(d)]
        out_shape = [jax.ShapeDtypeStruct((t, d), F32), jax.ShapeDtypeStruct((t, d), BF16)]
    return pl.pallas_call(
        functools.partial(_combine_kernel, bm=bm, final=final),
        grid=(nt,),
        in_specs=[pl.BlockSpec((1, 1, bm * TOP_K), lambda i: (i, 0, 0), memory_space=pltpu.SMEM),
                  pl.BlockSpec((1, 1, bm * TOP_K), lambda i: (jnp.minimum(i + 1, nt - 1), 0, 0),
                               memory_space=pltpu.SMEM),
                  row(LANES), row(d), vec, pl.BlockSpec((1, d), lambda i: (0, 0)), vec, vec,
                  pl.BlockSpec(memory_space=pl.ANY)],
        out_specs=out_specs,
        out_shape=out_shape,
        scratch_shapes=[pltpu.VMEM((2, TOP_K, bm * SUBLANES, LANES), U32), pltpu.SemaphoreType.DMA((2,))],
        compiler_params=_params("arbitrary"),
        name="moe_combine",
    )(dest3, dest3, tw, x, g2, ng, sc, sh, outp)


def _routing_tables(top_i, rank, counts_f):
    t = top_i.shape[0]
    n_asg = t * TOP_K
    counts = counts_f.astype(I32)
    padded = ((counts + MOE_BLOCK - 1) // MOE_BLOCK) * MOE_BLOCK
    pend = jnp.cumsum(padded)
    pstart = pend - padded
    experts = jnp.arange(N_EXPERTS, dtype=I32)
    start_of = jnp.sum(jnp.where(top_i[:, :, None] == experts, pstart, 0), axis=-1)
    dest = (start_of + rank).astype(I32)
    n_rows = ((n_asg + N_EXPERTS * MOE_BLOCK + MOE_BLOCK - 1) // MOE_BLOCK) * MOE_BLOCK
    n_blk = n_rows // MOE_BLOCK
    first_row = jnp.arange(n_blk, dtype=I32) * MOE_BLOCK
    block_e = jnp.minimum(jnp.sum((pend[None, :] <= first_row[:, None]).astype(I32), axis=1), N_EXPERTS - 1)
    n_used = (pend[-1:] // MOE_BLOCK).astype(I32)
    pad_lo = (pstart + counts).astype(I32)
    seg_end = jnp.sum(jnp.where(block_e[:, None] == experts, pad_lo, 0), axis=-1)
    block_valid = jnp.clip(seg_end - first_row, 0, MOE_BLOCK).astype(I32)
    return dest, block_e, n_used, block_valid, n_rows, pad_lo, pend.astype(I32)


def _rotary_tables(positions):
    inv_freq = ROPE_THETA ** (-jnp.arange(0, ROT_DIM, 2, dtype=F32) / ROT_DIM)
    ang = positions.astype(F32).reshape(-1, 1) * inv_freq
    cos, sin = jnp.cos(ang), jnp.sin(ang)
    t = ang.shape[0]
    pad = jnp.zeros((t, DA_DK - ROT_DIM), F32)
    c64 = jnp.concatenate([cos, cos, pad + 1.0], axis=1)
    s1_64 = jnp.concatenate([jnp.zeros_like(sin), sin, pad], axis=1)
    s2_64 = jnp.concatenate([-sin, jnp.zeros_like(sin), pad], axis=1)
    rep = LANES // DA_DK
    return jnp.tile(c64, (1, rep)), jnp.tile(s1_64, (1, rep)), jnp.tile(s2_64, (1, rep)), cos.T, sin.T


def kernel(x, c, positions, w_ada, b_ada, norm_mix, norm_ffn, w_in, diff_lambda, diff_subln, gmlp_ln_g, gmlp_ln_b, gmlp_w_spatial, gmlp_b_spatial, pool_w, pool_scale, conv_w, w_branch, w_gate, b_gate, w_out, router_w, router_b, expert_w_gu, expert_b_gu, expert_w_down, expert_b_down, final_norm):
    bsz, seq, d = x.shape
    depth = w_ada.shape[0]
    t = bsz * seq
    assert seq % 512 == 0 and d == 2 * MIX_W and d // 2 == SUBLANES * LANES

    rot_c, rot_s1, rot_s2, cos_t, sin_t = _rotary_tables(positions)
    c_pad = jnp.zeros((SUBLANES, d), F32).at[:bsz].set(c)
    mod = _adaln(c_pad, w_ada, b_ada)[:, :bsz].reshape(depth, bsz, 6, 1, d)
    rw_pad = jnp.zeros((depth, d, LANES), F32).at[:, :, :N_EXPERTS].set(router_w)
    rb_pad = jnp.full((depth, 1, LANES), NEG_INF, F32).at[:, 0, :N_EXPERTS].set(router_b)

    w_in_b, w_gate_b, w_branch_b, w_out_b = (w.astype(BF16) for w in (w_in, w_gate, w_branch, w_out))

    xf = x.reshape(t, d)
    h = None
    for l in range(depth):
        lam_init = 0.8 - 0.6 * math.exp(-0.3 * l)
        sh1, sc1, g1, sh2, sc2, g2 = (mod[l, :, i] for i in range(6))
        if l == 0:
            h = _norm_mod(xf, norm_mix[l].reshape(1, d), sc1, sh1, seq)
        z, qt, vt = _inproj(h, w_in_b, l, rot_c, rot_s1, rot_s2, cos_t, sin_t)
        y_a, w_gu_b, w_dn_b = _attention(z, qt, vt, diff_lambda[l], diff_subln[l].reshape(DA_DV, 1),
                                         expert_w_gu, expert_w_down, l, bsz, seq, lam_init)
        bs_b = jnp.broadcast_to(gmlp_b_spatial[l][:, :, None], (GM_GROUPS, GM_CHUNK, GM_CHUNK))
        y_b = _gmlp(z, gmlp_ln_g[l].reshape(1, MIX_W), gmlp_ln_b[l].reshape(1, MIX_W),
                    gmlp_w_spatial[l], bs_b)
        y_c = _pool(z, pool_w[l].astype(BF16), pool_scale[l].reshape(1, MIX_W), seq)
        y_d = _conv(z, conv_w[l], seq)
        merged = _merge(h, (y_a, y_b, y_c, y_d), w_gate_b, b_gate[l], w_branch_b, l)
        xf, hp, ti, tw, tr, cnt = _outproj(merged, w_out_b, l, xf, g1, norm_ffn[l].reshape(1, d),
                                           sc2, sh2, rw_pad[l], rb_pad[l], seq)
        dest, block_e, n_used, block_valid, n_rows, pad_lo, pad_hi = _routing_tables(
            ti[:, :TOP_K], tr[:, :TOP_K], cnt[0, :N_EXPERTS])
        xb = _dispatch(pad_lo, pad_hi, n_used, dest, hp, n_rows)
        outp = _experts(block_e, n_used, block_valid, xb, w_gu_b, expert_b_gu[l], w_dn_b, expert_b_down[l])
        if l + 1 < depth:
            nsh1, nsc1 = mod[l + 1, :, 0], mod[l + 1, :, 1]
            xf, h = _combine(dest, tw, xf, g2, norm_mix[l + 1].reshape(1, d), nsc1, nsh1, outp, seq, False)
        else:
            zero = jnp.zeros((bsz, 1, d), F32)
            (out,) = _combine(dest, tw, xf, g2, final_norm.reshape(1, d), zero, zero, outp, seq, True)
    return out.reshape(bsz, seq, d)
```

```python
import functools
import math

import jax
import jax.numpy as jnp
from jax import lax
from jax.experimental import pallas as pl
from jax.experimental.pallas import tpu as pltpu

F32 = jnp.float32
BF16 = jnp.bfloat16
U32 = jnp.uint32
I32 = jnp.int32

MIX_W = 1024
DA_HEADS = 8
DA_DK = 64
DA_DV = 128
ROT_DIM = 16
ROPE_THETA = 500000.0
GM_GROUPS = 8
GM_CHUNK = 128
POOL_WINDOWS = (2, 4, 8, 16)
POOL_GC = 256
N_BRANCH = 4
N_EXPERTS = 32
TOP_K = 4
D_EXPERT = 1024
SWIGLU_LIMIT = 7.0
SWIGLU_ALPHA = 1.702
MOE_BLOCK = 512
EPS = 1e-6
NEG_INF = -1e30

LANES = 128
SUBLANES = 8
HALO = 16
ONES_ROWS = 16
VMEM_LIMIT = 56 * 1024 * 1024

COL_K, COL_GU, COL_GV, COL_POOL, COL_CB, COL_CC, COL_CH = range(7)
IN_Q, IN_K, IN_V = 0, 1, 2


def _params(*sem):
    return pltpu.CompilerParams(dimension_semantics=sem, vmem_limit_bytes=VMEM_LIMIT)


def _rms_mod(x, g, sc, sh):
    y = x * lax.rsqrt(jnp.mean(x * x, axis=-1, keepdims=True) + EPS) * g
    return y * (1.0 + sc) + sh


def _pack_pair(lo, hi):
    lo_bits = lax.bitcast_convert_type(lo.astype(BF16).astype(F32), U32) >> 16
    hi_bits = lax.bitcast_convert_type(hi.astype(BF16).astype(F32), U32) & jnp.uint32(0xFFFF0000)
    return hi_bits | lo_bits


def _unpack_pair(w):
    lo = lax.bitcast_convert_type(w << 16, F32)
    hi = lax.bitcast_convert_type(w & jnp.uint32(0xFFFF0000), F32)
    return lo, hi


def _store_row_tiles(ref, packed):
    n = packed.shape[0]
    for c in range(SUBLANES):
        ref[pl.ds(c, n, stride=SUBLANES), :] = packed[:, c * LANES:(c + 1) * LANES]


def _load_row_tile_col(ref, c, n):
    return ref[pl.ds(c, n, stride=SUBLANES), :]


def _adaln_kernel(c_ref, w_ref, b_ref, o_ref):
    c = c_ref[...]
    o_ref[...] = jnp.dot(c * jax.nn.sigmoid(c), w_ref[...], preferred_element_type=F32) + b_ref[...]


def _adaln(c_pad, w_ada, b_ada):
    depth, d, n = w_ada.shape
    bn = 1536
    rows = c_pad.shape[0]
    return pl.pallas_call(
        _adaln_kernel,
        grid=(depth, n // bn),
        in_specs=[pl.BlockSpec((rows, d), lambda l, j: (0, 0)),
                  pl.BlockSpec((None, d, bn), lambda l, j: (l, 0, j)),
                  pl.BlockSpec((None, 1, bn), lambda l, j: (l, 0, j))],
        out_specs=pl.BlockSpec((None, rows, bn), lambda l, j: (l, 0, j)),
        out_shape=jax.ShapeDtypeStruct((depth, rows, n), F32),
        compiler_params=_params("arbitrary", "arbitrary"),
        name="adaln",
    )(c_pad, w_ada, b_ada.reshape(depth, 1, n))


def _norm_kernel(x_ref, g_ref, sc_ref, sh_ref, h_ref):
    h_ref[...] = _rms_mod(x_ref[...], g_ref[...], sc_ref[...], sh_ref[...]).astype(BF16)


def _norm_mod(x, g, sc, sh, seq):
    t, d = x.shape
    bm = 512
    tps = seq // bm
    vec = pl.BlockSpec((None, 1, d), lambda i: (i // tps, 0, 0))
    return pl.pallas_call(
        _norm_kernel,
        grid=(t // bm,),
        in_specs=[pl.BlockSpec((bm, d), lambda i: (i, 0)),
                  pl.BlockSpec((1, d), lambda i: (0, 0)), vec, vec],
        out_specs=pl.BlockSpec((bm, d), lambda i: (i, 0)),
        out_shape=jax.ShapeDtypeStruct((t, d), BF16),
        compiler_params=_params("arbitrary"),
        name="norm_mod",
    )(x, g, sc, sh)


def _inproj_kernel(h_ref, w_ref, c_ref, s1_ref, s2_ref, ct_ref, st_ref, z_ref, qt_ref, vt_ref):
    j = pl.program_id(1)
    acc = jnp.dot(h_ref[...], w_ref[...], preferred_element_type=F32)
    n_tiles = acc.shape[1] // LANES
    half = ROT_DIM // 2
    z_ref[...] = acc.astype(BF16)

    @pl.when(j == IN_Q)
    def _():
        scale = DA_DK ** -0.5 * math.log2(math.e)
        cos = ct_ref[...] * scale
        sin = st_ref[...] * scale
        for t in range(n_tiles):
            xt = acc[:, t * LANES:(t + 1) * LANES].T
            parts = []
            for comp in range(LANES // DA_DK):
                base = comp * DA_DK
                x1 = xt[base:base + half]
                x2 = xt[base + half:base + ROT_DIM]
                parts += [x1 * cos - x2 * sin, x2 * cos + x1 * sin, xt[base + ROT_DIM:base + DA_DK] * scale]
            qt_ref[t * LANES:(t + 1) * LANES, :] = jnp.concatenate(parts, axis=0).astype(BF16)

    @pl.when(j == IN_K)
    def _():
        for t in range(n_tiles):
            xt = acc[:, t * LANES:(t + 1) * LANES]
            r = (xt * c_ref[...] + pltpu.roll(xt, half, 1) * s1_ref[...]
                 + pltpu.roll(xt, LANES - half, 1) * s2_ref[...])
            z_ref[:, t * LANES:(t + 1) * LANES] = r.astype(BF16)

    @pl.when(j == IN_V)
    def _():
        for t in range(n_tiles):
            vt_ref[t * LANES:(t + 1) * LANES, :] = acc[:, t * LANES:(t + 1) * LANES].T.astype(BF16)


def _inproj(h, w_in, layer, rot_c, rot_s1, rot_s2, cos_t, sin_t):
    t, d = h.shape
    n = w_in.shape[2]
    bm, bn = 1024, MIX_W
    nz = n // bn - 2
    rot = pl.BlockSpec((bm, LANES), lambda i, j: (i, 0))
    rot_t = pl.BlockSpec((ROT_DIM // 2, bm), lambda i, j: (0, i))
    z_col = lambda j: jnp.where(j <= IN_K, 0, jnp.maximum(j - 2, 1))
    return pl.pallas_call(
        _inproj_kernel,
        grid=(t // bm, n // bn),
        in_specs=[pl.BlockSpec((bm, d), lambda i, j: (i, 0)),
                  pl.BlockSpec((None, d, bn), lambda i, j: (layer, 0, j)), rot, rot, rot, rot_t, rot_t],
        out_specs=[pl.BlockSpec((bm, bn), lambda i, j: (i, z_col(j))),
                   pl.BlockSpec((bn, bm), lambda i, j: (0, i)),
                   pl.BlockSpec((bn, bm), lambda i, j: (0, i))],
        out_shape=[jax.ShapeDtypeStruct((t, nz * bn), BF16),
                   jax.ShapeDtypeStruct((bn, t), BF16),
                   jax.ShapeDtypeStruct((bn, t), BF16)],
        compiler_params=_params("arbitrary", "arbitrary"),
        name="inproj",
    )(h, w_in, rot_c, rot_s1, rot_s2, cos_t, sin_t)


def _attn_kernel(qt_ref, k_ref, vt_ref, lam_ref, sub_ref, wgu_ref, wdn_ref, o_ref, wgu_o_ref, wdn_o_ref,
                 sa_ref, sb_ref, ma_sc, mb_sc, m_sc, acc_sc, *, tq, lam_init):
    i = pl.program_id(2)
    wgu_o_ref[...] = wgu_ref[...].astype(BF16)
    wdn_o_ref[...] = wdn_ref[...].astype(BF16)
    qt = qt_ref[...]
    row = lax.broadcasted_iota(I32, qt.shape, 0)
    zero = jnp.zeros_like(qt)
    qs = jnp.concatenate([jnp.where(row < DA_DK, qt, zero), jnp.where(row >= DA_DK, qt, zero)], axis=1)
    m_sc[...] = jnp.full(m_sc.shape, NEG_INF, F32)
    acc_sc[...] = jnp.zeros(acc_sc.shape, F32)

    def scores(j, s_ref, smax_ref):
        off = pl.multiple_of(j * tq, tq)
        s = jnp.dot(k_ref[pl.ds(off, tq), :], qs, preferred_element_type=F32)
        s_ref[...] = s
        smax_ref[...] = jnp.max(s, axis=0, keepdims=True)

    def softmax_pv(j, s_ref, smax_ref, masked):
        off = pl.multiple_of(j * tq, tq)
        s = s_ref[...]
        if masked:
            key = lax.broadcasted_iota(I32, s.shape, 0)
            c2 = lax.broadcasted_iota(I32, s.shape, 1)
            s = jnp.where(key <= jnp.where(c2 >= tq, c2 - tq, c2), s, NEG_INF)
            s_max = jnp.max(s, axis=0, keepdims=True)
        else:
            s_max = smax_ref[...]
        m_prev = m_sc[...]
        m_new = jnp.maximum(m_prev, s_max)
        alpha = jnp.exp2(m_prev - m_new)
        p = jnp.exp2(s - m_new).astype(BF16)
        vt_aug = jnp.concatenate([vt_ref[:, pl.ds(off, tq)], jnp.ones((ONES_ROWS, tq), BF16)], axis=0)
        acc_sc[...] = alpha * acc_sc[...] + jnp.dot(vt_aug, p, preferred_element_type=F32)
        m_sc[...] = m_new

    scores(0, sa_ref, ma_sc)

    def pair(j):
        scores(j + 1, sb_ref, mb_sc)
        softmax_pv(j, sa_ref, ma_sc, False)
        scores(j + 2, sa_ref, ma_sc)
        softmax_pv(j + 1, sb_ref, mb_sc, False)

    def octet(jj, carry):
        for q in range(4):
            pair(8 * jj + 2 * q)
        return carry

    n_oct = i // 8
    lax.fori_loop(0, n_oct, octet, 0)
    done = 8 * n_oct
    rest = i - done

    @pl.when(rest >= 4)
    def _():
        pair(done)
        pair(done + 2)

    done = done + jnp.where(rest >= 4, 4, 0)

    @pl.when(i - done >= 2)
    def _():
        pair(done)

    @pl.when(i % 2 == 1)
    def _():
        scores(i, sb_ref, mb_sc)
        softmax_pv(i - 1, sa_ref, ma_sc, False)
        softmax_pv(i, sb_ref, mb_sc, True)

    @pl.when(i % 2 == 0)
    def _():
        softmax_pv(i, sa_ref, ma_sc, True)

    lv = lam_ref[...]
    lam = (jnp.exp(jnp.sum(lv[0:1] * lv[1:2], axis=-1, keepdims=True))
           - jnp.exp(jnp.sum(lv[2:3] * lv[3:4], axis=-1, keepdims=True)) + lam_init)
    acc = acc_sc[:DA_DV, :]
    l = acc_sc[DA_DV:DA_DV + 1, :]
    o = acc[:, :tq] / l[:, :tq] - lam * (acc[:, tq:] / l[:, tq:])
    y = o * lax.rsqrt(jnp.mean(o * o, axis=0, keepdims=True) + EPS) * sub_ref[...] * (1.0 - lam_init)
    o_ref[...] = y.T.astype(o_ref.dtype)


def _attention(z, qt, vt, lam_vecs, subln_col, w_gu, w_dn, layer, bsz, seq, lam_init):
    t = z.shape[0]
    tq = min(512, seq)
    nq = seq // tq
    steps = bsz * DA_HEADS * nq
    _, ne, d, gu_w = w_gu.shape
    gu_rows, dn_rows = ne * d // steps, ne * D_EXPERT // steps
    assert gu_rows * steps == ne * d and dn_rows * steps == ne * D_EXPERT and dn_rows % HALO == 0
    step = lambda b, h, i: (b * DA_HEADS + h) * nq + i
    y_a, gu_b, dn_b = pl.pallas_call(
        functools.partial(_attn_kernel, tq=tq, lam_init=lam_init),
        grid=(bsz, DA_HEADS, nq),
        in_specs=[pl.BlockSpec((DA_DV, tq), lambda b, h, i: (h, b * nq + i)),
                  pl.BlockSpec((seq, DA_DV), lambda b, h, i: (b, COL_K * DA_HEADS + h)),
                  pl.BlockSpec((DA_DV, seq), lambda b, h, i: (h, b)),
                  pl.BlockSpec((4, DA_DK), lambda b, h, i: (0, 0)),
                  pl.BlockSpec((DA_DV, 1), lambda b, h, i: (0, 0)),
                  pl.BlockSpec((None, gu_rows, gu_w), lambda b, h, i: (layer, step(b, h, i), 0)),
                  pl.BlockSpec((None, dn_rows, d), lambda b, h, i: (layer, step(b, h, i), 0))],
        out_specs=[pl.BlockSpec((tq, DA_DV), lambda b, h, i: (b * nq + i, h)),
                   pl.BlockSpec((gu_rows, gu_w), lambda b, h, i: (step(b, h, i), 0)),
                   pl.BlockSpec((dn_rows, d), lambda b, h, i: (step(b, h, i), 0))],
        out_shape=[jax.ShapeDtypeStruct((t, MIX_W), BF16),
                   jax.ShapeDtypeStruct((ne * d, gu_w), BF16),
                   jax.ShapeDtypeStruct((ne * D_EXPERT, d), BF16)],
        scratch_shapes=[pltpu.VMEM((tq, 2 * tq), F32), pltpu.VMEM((tq, 2 * tq), F32),
                        pltpu.VMEM((1, 2 * tq), F32), pltpu.VMEM((1, 2 * tq), F32),
                        pltpu.VMEM((1, 2 * tq), F32), pltpu.VMEM((DA_DV + ONES_ROWS, 2 * tq), F32)],
        compiler_params=_params("arbitrary", "arbitrary", "arbitrary"),
        name="diff_attn",
    )(qt, z, vt, lam_vecs, subln_col, w_gu.reshape(w_gu.shape[0], ne * d, gu_w),
      w_dn.reshape(w_dn.shape[0], ne * D_EXPERT, d))
    return y_a, gu_b.reshape(ne, d, gu_w), dn_b.reshape(ne, D_EXPERT, d)


def _gelu(x):
    return 0.5 * x * (1.0 + lax.erf(x * math.sqrt(0.5)))


def _gmlp_kernel(u_ref, v_ref, g_ref, b_ref, ws_ref, bs_ref, o_ref, *, bm):
    v = _gelu(v_ref[...].astype(F32))
    mu = jnp.mean(v, axis=-1, keepdims=True)
    vc = v - mu
    var = jnp.mean(vc * vc, axis=-1, keepdims=True)
    vn = (vc * lax.rsqrt(var + EPS) * g_ref[...] + b_ref[...]).astype(BF16)
    r = lax.broadcasted_iota(I32, (GM_CHUNK, GM_CHUNK), 0)
    c = lax.broadcasted_iota(I32, (GM_CHUNK, GM_CHUNK), 1)
    for g in range(GM_GROUPS):
        cols = slice(g * GM_CHUNK, (g + 1) * GM_CHUNK)
        wm = jnp.where(c <= r, ws_ref[g], 0.0).astype(BF16)
        bias = bs_ref[g]
        for n in range(bm // GM_CHUNK):
            rows = slice(n * GM_CHUNK, (n + 1) * GM_CHUNK)
            sv = jnp.dot(wm, vn[rows, cols], preferred_element_type=F32) + bias
            u = _gelu(u_ref[rows, cols].astype(F32))
            o_ref[rows, cols] = (u * sv).astype(o_ref.dtype)


def _gmlp(z, ln_g, ln_b, ws, bs_b):
    t = z.shape[0]
    bm = 512
    full = lambda shape: pl.BlockSpec(shape, lambda i: (0,) * len(shape))
    return pl.pallas_call(
        functools.partial(_gmlp_kernel, bm=bm),
        grid=(t // bm,),
        in_specs=[pl.BlockSpec((bm, MIX_W), lambda i: (i, COL_GU)),
                  pl.BlockSpec((bm, MIX_W), lambda i: (i, COL_GV)),
                  full((1, MIX_W)), full((1, MIX_W)),
                  full((GM_GROUPS, GM_CHUNK, GM_CHUNK)), full((GM_GROUPS, GM_CHUNK, GM_CHUNK))],
        out_specs=pl.BlockSpec((bm, MIX_W), lambda i: (i, 0)),
        out_shape=jax.ShapeDtypeStruct((t, MIX_W), BF16),
        compiler_params=_params("arbitrary"),
        name="gmlp",
    )(z, z, ln_g, ln_b, ws, bs_b)


def _pool_kernel(z_ref, halo_ref, pw_ref, ps_ref, o_ref, band_sc, bandh_sc, *, bm, tps):
    p0 = (pl.program_id(0) % tps) * bm

    @pl.when(pl.program_id(0) == 0)
    def _():
        d = (lax.broadcasted_iota(I32, (bm, bm), 0) - lax.broadcasted_iota(I32, (bm, bm), 1))
        dh = (lax.broadcasted_iota(I32, (bm, HALO), 0) + HALO - lax.broadcasted_iota(I32, (bm, HALO), 1))
        for g, w in enumerate(POOL_WINDOWS):
            band_sc[g] = jnp.where(d >= 0, jnp.where(d < w, 1.0, 0.0), 0.0).astype(BF16)
            bandh_sc[g] = jnp.where(dh < w, 1.0, 0.0).astype(BF16)

    pos = p0 + lax.broadcasted_iota(I32, (bm, 1), 0)
    for g, w in enumerate(POOL_WINDOWS):
        cols = slice(g * POOL_GC, (g + 1) * POOL_GC)
        x = z_ref[:, cols]
        hx = halo_ref[:, cols]
        hx = jnp.where(p0 > 0, hx, jnp.zeros_like(hx))
        win = (jnp.dot(band_sc[g], x, preferred_element_type=F32)
               + jnp.dot(bandh_sc[g], hx, preferred_element_type=F32))
        cnt = jnp.minimum(pos + 1, w).astype(F32)
        pooled = (win / cnt - x.astype(F32)).astype(BF16)
        mixed = jnp.dot(pooled, pw_ref[g], preferred_element_type=F32)
        o_ref[:, cols] = (mixed * ps_ref[:, cols]).astype(o_ref.dtype)


def _halo_spec(bm, col):
    return pl.BlockSpec((HALO, MIX_W), lambda i: (jnp.maximum(i * (bm // HALO) - 1, 0), col))


def _pool(z, pw, ps, seq):
    t = z.shape[0]
    bm = min(512, seq)
    ng = len(POOL_WINDOWS)
    return pl.pallas_call(
        functools.partial(_pool_kernel, bm=bm, tps=seq // bm),
        grid=(t // bm,),
        in_specs=[pl.BlockSpec((bm, MIX_W), lambda i: (i, COL_POOL)),
                  _halo_spec(bm, COL_POOL),
                  pl.BlockSpec((ng, POOL_GC, POOL_GC), lambda i: (0, 0, 0)),
                  pl.BlockSpec((1, MIX_W), lambda i: (0, 0))],
        out_specs=pl.BlockSpec((bm, MIX_W), lambda i: (i, 0)),
        out_shape=jax.ShapeDtypeStruct((t, MIX_W), BF16),
        scratch_shapes=[pltpu.VMEM((ng, bm, bm), BF16), pltpu.VMEM((ng, bm, HALO), BF16)],
        compiler_params=_params("arbitrary"),
        name="pool",
    )(z, z, pw, ps)


def _conv_kernel(b_ref, c_ref, h_ref, hc_ref, hh_ref, cw_ref, o_ref, *, tps):
    first = (pl.program_id(0) % tps) == 0
    xin = c_ref[...].astype(F32) * h_ref[...].astype(F32)
    hal = hc_ref[...].astype(F32) * hh_ref[...].astype(F32)
    hal = jnp.where(first, 0.0, hal)
    hm1 = hal[HALO - 1:HALO, :]
    hm2 = hal[HALO - 2:HALO - 1, :]
    row = lax.broadcasted_iota(I32, xin.shape, 0)
    s1 = jnp.where(row == 0, hm1, pltpu.roll(xin, 1, 0))
    s2 = jnp.where(row == 0, hm2, jnp.where(row == 1, hm1, pltpu.roll(xin, 2, 0)))
    cw = cw_ref[...]
    y = cw[0:1] * s2 + cw[1:2] * s1 + cw[2:3] * xin
    o_ref[...] = (b_ref[...].astype(F32) * y).astype(o_ref.dtype)


def _conv(z, cw, seq):
    t = z.shape[0]
    bm = min(512, seq)
    blk = lambda col: pl.BlockSpec((bm, MIX_W), lambda i: (i, col))
    return pl.pallas_call(
        functools.partial(_conv_kernel, tps=seq // bm),
        grid=(t // bm,),
        in_specs=[blk(COL_CB), blk(COL_CC), blk(COL_CH), _halo_spec(bm, COL_CC), _halo_spec(bm, COL_CH),
                  pl.BlockSpec(cw.shape, lambda i: (0, 0))],
        out_specs=pl.BlockSpec((bm, MIX_W), lambda i: (i, 0)),
        out_shape=jax.ShapeDtypeStruct((t, MIX_W), BF16),
        compiler_params=_params("arbitrary"),
        name="short_conv",
    )(z, z, z, z, z, cw)


def _merge_kernel(h_ref, ya_ref, yb_ref, yc_ref, yd_ref, wg_ref, bg_ref, wb_ref, o_ref):
    h = h_ref[...]
    bg = bg_ref[...]
    merged = None
    for i, y_ref in enumerate((ya_ref, yb_ref, yc_ref, yd_ref)):
        gate = jax.nn.sigmoid(jnp.dot(h, wg_ref[i], preferred_element_type=F32) + bg[i:i + 1])
        term = gate * jnp.dot(y_ref[...], wb_ref[i], preferred_element_type=F32)
        merged = term if merged is None else merged + term
    o_ref[...] = merged.astype(o_ref.dtype)


def _merge(h, ys, wg, bg, wb, layer):
    t, d = h.shape
    bm, bn = 1024, 256
    yspec = pl.BlockSpec((bm, MIX_W), lambda i, j: (i, 0))
    return pl.pallas_call(
        _merge_kernel,
        grid=(t // bm, d // bn),
        in_specs=[pl.BlockSpec((bm, d), lambda i, j: (i, 0)), yspec, yspec, yspec, yspec,
                  pl.BlockSpec((None, N_BRANCH, d, bn), lambda i, j: (layer, 0, 0, j)),
                  pl.BlockSpec((N_BRANCH, bn), lambda i, j: (0, j)),
                  pl.BlockSpec((None, N_BRANCH, MIX_W, bn), lambda i, j: (layer, 0, 0, j))],
        out_specs=pl.BlockSpec((bm, bn), lambda i, j: (i, j)),
        out_shape=jax.ShapeDtypeStruct((t, d), BF16),
        compiler_params=_params("arbitrary", "arbitrary"),
        name="branch_merge",
    )(h, *ys, wg, bg, wb)


def _outproj_kernel(m_ref, wo_ref, x_ref, g1_ref, ng_ref, sc_ref, sh_ref, rw_ref, rb_ref,
                    xo_ref, hp_ref, ti_ref, tw_ref, tr_ref, cnt_ref, carry_sc):
    out = jnp.dot(m_ref[...], wo_ref[...], preferred_element_type=F32)
    x_new = x_ref[...] + g1_ref[...] * out
    xo_ref[...] = x_new
    h2 = _rms_mod(x_new, ng_ref[...], sc_ref[...], sh_ref[...])
    half = h2.shape[1] // 2
    _store_row_tiles(hp_ref, _pack_pair(h2[:, :half], h2[:, half:]))

    logits = jnp.dot(h2, rw_ref[...], preferred_element_type=F32) + rb_ref[...]
    lane = lax.broadcasted_iota(I32, logits.shape, 1).astype(F32)
    vals, idxs = [], []
    cur = logits
    for _ in range(TOP_K):
        m = jnp.max(cur, axis=-1, keepdims=True)
        idx = jnp.min(jnp.where(cur == m, lane, float(LANES)), axis=-1, keepdims=True)
        vals.append(m)
        idxs.append(idx)
        cur = jnp.where(lane == idx, -3e38, cur)
    es = [jnp.exp(v - vals[0]) for v in vals]
    den = es[0] + es[1] + es[2] + es[3]

    @pl.when(pl.program_id(0) == 0)
    def _():
        carry_sc[...] = jnp.zeros(carry_sc.shape, F32)

    onehot = jnp.zeros(logits.shape, F32)
    for k in range(TOP_K):
        onehot = onehot + jnp.where(lane == idxs[k], 1.0, 0.0)
    bm = logits.shape[0]
    tri = jnp.where(lax.broadcasted_iota(I32, (bm, bm), 0) > lax.broadcasted_iota(I32, (bm, bm), 1),
                    1.0, 0.0).astype(BF16)
    before = jnp.dot(tri, onehot.astype(BF16), preferred_element_type=F32) + carry_sc[...]
    carry = carry_sc[...] + jnp.sum(onehot, axis=0, keepdims=True)
    carry_sc[...] = carry
    cnt_ref[...] = jnp.broadcast_to(carry, cnt_ref.shape)

    ti = jnp.zeros(logits.shape, F32)
    tw = jnp.zeros(logits.shape, F32)
    tr = jnp.zeros(logits.shape, F32)
    for k in range(TOP_K):
        rank = jnp.sum(jnp.where(lane == idxs[k], before, 0.0), axis=-1, keepdims=True)
        ti = jnp.where(lane == float(k), idxs[k], ti)
        tw = jnp.where(lane == float(k), es[k] / den, tw)
        tr = jnp.where(lane == float(k), rank, tr)
    ti_ref[...] = ti.astype(I32)
    tw_ref[...] = tw
    tr_ref[...] = tr.astype(I32)


def _outproj(merged, wo, layer, x, g1, ng, sc, sh, rw, rb, seq):
    t, d = x.shape
    bm = 512
    tps = seq // bm
    row = lambda w: pl.BlockSpec((bm, w), lambda i: (i, 0))
    vec = pl.BlockSpec((None, 1, d), lambda i: (i // tps, 0, 0))
    full = lambda a: pl.BlockSpec(a.shape, lambda i: (0, 0))
    return pl.pallas_call(
        _outproj_kernel,
        grid=(t // bm,),
        in_specs=[row(d), pl.BlockSpec((None, d, d), lambda i: (layer, 0, 0)), row(d), vec, full(ng), vec, vec,
                  full(rw), full(rb)],
        out_specs=[row(d), pl.BlockSpec((bm * SUBLANES, LANES), lambda i: (i, 0)),
                   row(LANES), row(LANES), row(LANES),
                   pl.BlockSpec((SUBLANES, LANES), lambda i: (0, 0))],
        out_shape=[jax.ShapeDtypeStruct((t, d), F32), jax.ShapeDtypeStruct((t * SUBLANES, LANES), U32),
                   jax.ShapeDtypeStruct((t, LANES), I32), jax.ShapeDtypeStruct((t, LANES), F32),
                   jax.ShapeDtypeStruct((t, LANES), I32), jax.ShapeDtypeStruct((SUBLANES, LANES), F32)],
        scratch_shapes=[pltpu.VMEM((1, LANES), F32)],
        compiler_params=_params("arbitrary"),
        name="outproj_router",
    )(merged, wo, x, g1, ng, sc, sh, rw, rb)


def _row_copy(src_ref, src_row, dst_ref, dst_row, sem):
    return pltpu.make_async_copy(src_ref.at[pl.ds(pl.multiple_of(src_row * SUBLANES, SUBLANES), SUBLANES), :],
                                 dst_ref.at[pl.ds(pl.multiple_of(dst_row * SUBLANES, SUBLANES), SUBLANES), :],
                                 sem)


def _dispatch_kernel(pad_lo_ref, pad_hi_ref, nu_ref, dest_ref, hp_ref, xb_hbm, zeros_sc, sem, zsem,
                     *, bm, n_blk):
    blk_rows = MOE_BLOCK * SUBLANES

    @pl.when(pl.program_id(0) == 0)
    def _():
        zeros_sc[...] = jnp.zeros(zeros_sc.shape, zeros_sc.dtype)

        def pad_rows(act):
            def per_expert(e, carry):
                row = pad_lo_ref[e]
                n_pad = pad_hi_ref[e] - row
                p = MOE_BLOCK // 2
                while p >= 1:
                    take = (n_pad & p) != 0

                    @pl.when(take)
                    def _(row=row, p=p):
                        dst = xb_hbm.at[pl.ds(pl.multiple_of(row * SUBLANES, SUBLANES), p * SUBLANES), :]
                        act(pltpu.make_async_copy(zeros_sc.at[pl.ds(0, p * SUBLANES), :], dst, zsem))

                    row = row + jnp.where(take, p, 0)
                    p //= 2
                return carry
            lax.fori_loop(0, N_EXPERTS, per_expert, 0)

        def tail_blocks(act):
            def per_block(b, carry):
                off = pl.multiple_of(b * blk_rows, blk_rows)
                act(pltpu.make_async_copy(zeros_sc, xb_hbm.at[pl.ds(off, blk_rows), :], zsem))
                return carry
            lax.fori_loop(nu_ref[0], n_blk, per_block, 0)

        pad_rows(lambda cp: cp.start())
        tail_blocks(lambda cp: cp.start())
        pad_rows(lambda cp: cp.wait())
        tail_blocks(lambda cp: cp.wait())

    def issue(r, carry):
        for k in range(TOP_K):
            _row_copy(hp_ref, r, xb_hbm, dest_ref[0, 0, r * TOP_K + k], sem).start(priority=k % 2)
        return carry

    lax.fori_loop(0, bm, issue, 0, unroll=4)
    for k in range(TOP_K):
        pltpu.make_async_copy(hp_ref, xb_hbm.at[pl.ds(0, bm * SUBLANES), :], sem).wait()


def _dispatch(pad_lo, pad_hi, n_used, dest, hp, n_rows):
    t = dest.shape[0]
    bm = 512
    nt = t // bm
    n_blk = n_rows // MOE_BLOCK
    return pl.pallas_call(
        functools.partial(_dispatch_kernel, bm=bm, n_blk=n_blk),
        grid_spec=pltpu.PrefetchScalarGridSpec(
            num_scalar_prefetch=3,
            grid=(nt,),
            in_specs=[pl.BlockSpec((1, 1, bm * TOP_K), lambda i, *_: (i, 0, 0), memory_space=pltpu.SMEM),
                      pl.BlockSpec((bm * SUBLANES, LANES), lambda i, *_: (i, 0))],
            out_specs=pl.BlockSpec(memory_space=pl.ANY),
            scratch_shapes=[pltpu.VMEM((MOE_BLOCK * SUBLANES, LANES), U32),
                            pltpu.SemaphoreType.DMA(()), pltpu.SemaphoreType.DMA(())]),
        out_shape=jax.ShapeDtypeStruct((n_rows * SUBLANES, LANES), U32),
        compiler_params=_params("arbitrary"),
        name="moe_dispatch",
    )(pad_lo, pad_hi, n_used, dest.reshape(nt, 1, bm * TOP_K), hp)


def _expert_kernel(be_ref, nu_ref, bv_ref, first_ref, slot_ref, nxt_ref, xb_ref, wgu_hbm, bgu_ref, wdn_hbm,
                   bdn_ref, o_ref, wgu_buf, wdn_buf, sem):
    b = pl.program_id(0)
    valid = bv_ref[b]
    slot = slot_ref[b]

    def weight_copies(e, s):
        return (pltpu.make_async_copy(wgu_hbm.at[e], wgu_buf.at[s], sem.at[0, s]),
                pltpu.make_async_copy(wdn_hbm.at[e], wdn_buf.at[s], sem.at[1, s]))

    @pl.when(b == 0)
    def _():
        for cp in weight_copies(be_ref[0], 0):
            cp.start()

    @pl.when(first_ref[b] == 1)
    def _():
        @pl.when(nxt_ref[b] >= 0)
        def _():
            for cp in weight_copies(nxt_ref[b], 1 - slot):
                cp.start()

        for cp in weight_copies(be_ref[b], slot):
            cp.wait()

    wgu_ref = wgu_buf.at[slot]
    wdn_ref = wdn_buf.at[slot]

    def ffn(rows):
        los, his = [], []
        for c in range(SUBLANES):
            lo, hi = _unpack_pair(_load_row_tile_col(xb_ref, c, rows))
            los.append(lo.astype(BF16))
            his.append(hi.astype(BF16))
        lo = jnp.concatenate(los, axis=1)
        hi = jnp.concatenate(his, axis=1)
        half = lo.shape[1]
        gu = (jnp.dot(lo, wgu_ref[:half, :], preferred_element_type=F32)
              + jnp.dot(hi, wgu_ref[half:, :], preferred_element_type=F32) + bgu_ref[...])
        g = jnp.minimum(gu[:, :D_EXPERT], SWIGLU_LIMIT)
        u = jnp.clip(gu[:, D_EXPERT:], -SWIGLU_LIMIT, SWIGLU_LIMIT)
        act = (u + 1.0) * (g * jax.nn.sigmoid(SWIGLU_ALPHA * g))
        out = jnp.dot(act.astype(BF16), wdn_ref[...], preferred_element_type=F32) + bdn_ref[...]
        _store_row_tiles(o_ref, _pack_pair(out[:, :half], out[:, half:]))
        if rows < MOE_BLOCK:
            o_ref[rows * SUBLANES:, :] = jnp.zeros(((MOE_BLOCK - rows) * SUBLANES, LANES), o_ref.dtype)

    @pl.when(valid > MOE_BLOCK // 2)
    def _():
        ffn(MOE_BLOCK)

    @pl.when(jnp.logical_and(valid > 0, valid <= MOE_BLOCK // 2))
    def _():
        ffn(MOE_BLOCK // 2)

    @pl.when(valid == 0)
    def _():
        o_ref[...] = jnp.zeros(o_ref.shape, o_ref.dtype)


def _experts(block_e, n_used, block_valid, run_first, run_slot, run_next, xb, w_gu, b_gu, w_dn, b_dn):
    n_blk = xb.shape[0] // (MOE_BLOCK * SUBLANES)
    ne, d, _ = w_gu.shape
    blk = pl.BlockSpec((MOE_BLOCK * SUBLANES, LANES), lambda b, be, *_: (b, 0))
    blk_in = pl.BlockSpec((MOE_BLOCK * SUBLANES, LANES), lambda b, be, nu, *_: (jnp.minimum(b, nu[0] - 1), 0))
    return pl.pallas_call(
        _expert_kernel,
        grid_spec=pltpu.PrefetchScalarGridSpec(
            num_scalar_prefetch=6,
            grid=(n_blk,),
            in_specs=[blk_in,
                      pl.BlockSpec(memory_space=pl.ANY),
                      pl.BlockSpec((None, 1, 2 * D_EXPERT), lambda b, be, *_: (be[b], 0, 0)),
                      pl.BlockSpec(memory_space=pl.ANY),
                      pl.BlockSpec((None, 1, d), lambda b, be, *_: (be[b], 0, 0))],
            out_specs=blk,
            scratch_shapes=[pltpu.VMEM((2, d, 2 * D_EXPERT), BF16), pltpu.VMEM((2, D_EXPERT, d), BF16),
                            pltpu.SemaphoreType.DMA((2, 2))]),
        out_shape=jax.ShapeDtypeStruct(xb.shape, U32),
        compiler_params=_params("arbitrary"),
        name="moe_experts",
    )(block_e, n_used, block_valid, run_first, run_slot, run_next, xb, w_gu, b_gu.reshape(ne, 1, -1), w_dn,
      b_dn.reshape(ne, 1, -1))


def _combine_kernel(dest_ref, dnext_ref, tw_ref, x_ref, g2_ref, ng_ref, sc_ref, sh_ref, outp_hbm, *rest,
                    bm, final):
    if final:
        ho_ref, buf, sem = rest
    else:
        xo_ref, ho_ref, buf, sem = rest
    i = pl.program_id(0)
    slot = i % 2

    def gather_tile(d_ref, s):
        def issue(r, carry):
            for k in range(TOP_K):
                _row_copy(outp_hbm, d_ref[0, 0, r * TOP_K + k], buf.at[s, k], r, sem.at[s]).start(priority=k % 2)
            return carry

        lax.fori_loop(0, bm, issue, 0, unroll=4)

    @pl.when(i == 0)
    def _():
        gather_tile(dest_ref, 0)

    @pl.when(i + 1 < pl.num_programs(0))
    def _():
        gather_tile(dnext_ref, 1 - slot)

    for k in range(TOP_K):
        pltpu.make_async_copy(outp_hbm.at[pl.ds(0, bm * SUBLANES), :], buf.at[slot, k], sem.at[slot]).wait()

    tw = tw_ref[...]
    wk = [tw[:, k:k + 1] for k in range(TOP_K)]
    y_lo, y_hi = [], []
    for c in range(SUBLANES):
        a_lo = a_hi = None
        for k in range(TOP_K):
            lo, hi = _unpack_pair(buf[slot, k, pl.ds(c, bm, stride=SUBLANES), :])
            a_lo = wk[k] * lo if a_lo is None else a_lo + wk[k] * lo
            a_hi = wk[k] * hi if a_hi is None else a_hi + wk[k] * hi
        y_lo.append(a_lo)
        y_hi.append(a_hi)
    x_new = x_ref[...] + g2_ref[...] * jnp.concatenate(y_lo + y_hi, axis=1)
    if final:
        ho_ref[...] = x_new * lax.rsqrt(jnp.mean(x_new * x_new, axis=-1, keepdims=True) + EPS) * ng_ref[...]
    else:
        xo_ref[...] = x_new
        ho_ref[...] = _rms_mod(x_new, ng_ref[...], sc_ref[...], sh_ref[...]).astype(BF16)


def _combine(dest, tw, x, g2, ng, sc, sh, outp, seq, final):
    t, d = x.shape
    bm = 256
    tps = seq // bm
    nt = t // bm
    row = lambda w: pl.BlockSpec((bm, w), lambda i: (i, 0))
    vec = pl.BlockSpec((None, 1, d), lambda i: (i // tps, 0, 0))
    dest3 = dest.reshape(nt, 1, bm * TOP_K)
    if final:
        out_specs = [row(d)]
        out_shape = [jax.ShapeDtypeStruct((t, d), F32)]
    else:
        out_specs = [row(d), row(d)]
        out_shape = [jax.ShapeDtypeStruct((t, d), F32), jax.ShapeDtypeStruct((t, d), BF16)]
    return pl.pallas_call(
        functools.partial(_combine_kernel, bm=bm, final=final),
        grid=(nt,),
        in_specs=[pl.BlockSpec((1, 1, bm * TOP_K), lambda i: (i, 0, 0), memory_space=pltpu.SMEM),
                  pl.BlockSpec((1, 1, bm * TOP_K), lambda i: (jnp.minimum(i + 1, nt - 1), 0, 0),
                               memory_space=pltpu.SMEM),
                  row(LANES), row(d), vec, pl.BlockSpec((1, d), lambda i: (0, 0)), vec, vec,
                  pl.BlockSpec(memory_space=pl.ANY)],
        out_specs=out_specs,
        out_shape=out_shape,
        scratch_shapes=[pltpu.VMEM((2, TOP_K, bm * SUBLANES, LANES), U32), pltpu.SemaphoreType.DMA((2,))],
        compiler_params=_params("arbitrary"),
        name="moe_combine",
    )(dest3, dest3, tw, x, g2, ng, sc, sh, outp)


def _routing_tables(top_i, rank, counts_f):
    t = top_i.shape[0]
    n_asg = t * TOP_K
    counts = counts_f.astype(I32)
    padded = ((counts + MOE_BLOCK - 1) // MOE_BLOCK) * MOE_BLOCK
    pend = jnp.cumsum(padded)
    pstart = pend - padded
    experts = jnp.arange(N_EXPERTS, dtype=I32)
    start_of = jnp.sum(jnp.where(top_i[:, :, None] == experts, pstart, 0), axis=-1)
    dest = (start_of + rank).astype(I32)
    n_rows = ((n_asg + N_EXPERTS * MOE_BLOCK + MOE_BLOCK - 1) // MOE_BLOCK) * MOE_BLOCK
    n_blk = n_rows // MOE_BLOCK
    first_row = jnp.arange(n_blk, dtype=I32) * MOE_BLOCK
    block_e = jnp.minimum(jnp.sum((pend[None, :] <= first_row[:, None]).astype(I32), axis=1), N_EXPERTS - 1)
    n_used = (pend[-1:] // MOE_BLOCK).astype(I32)
    pad_lo = (pstart + counts).astype(I32)
    seg_end = jnp.sum(jnp.where(block_e[:, None] == experts, pad_lo, 0), axis=-1)
    block_valid = jnp.clip(seg_end - first_row, 0, MOE_BLOCK).astype(I32)
    blk = jnp.arange(n_blk, dtype=I32)
    used = blk < n_used[0]
    prev_e = jnp.concatenate([jnp.full((1,), -1, I32), block_e[:-1]])
    run_first = jnp.logical_and(block_e != prev_e, used).astype(I32)
    run_slot = ((jnp.cumsum(run_first) - 1) % 2).astype(I32)
    next_blk = jnp.sum(jnp.where(block_e[:, None] == experts, pend, 0), axis=-1) // MOE_BLOCK
    next_e = jnp.sum(jnp.where(next_blk[:, None] == blk[None, :], block_e[None, :], 0), axis=-1)
    run_next = jnp.where(next_blk < n_used[0], next_e, -1).astype(I32)
    return dest, block_e, n_used, block_valid, (run_first, run_slot, run_next), n_rows, pad_lo, pend.astype(I32)


def _rotary_tables(positions):
    inv_freq = ROPE_THETA ** (-jnp.arange(0, ROT_DIM, 2, dtype=F32) / ROT_DIM)
    ang = positions.astype(F32).reshape(-1, 1) * inv_freq
    cos, sin = jnp.cos(ang), jnp.sin(ang)
    t = ang.shape[0]
    pad = jnp.zeros((t, DA_DK - ROT_DIM), F32)
    c64 = jnp.concatenate([cos, cos, pad + 1.0], axis=1)
    s1_64 = jnp.concatenate([jnp.zeros_like(sin), sin, pad], axis=1)
    s2_64 = jnp.concatenate([-sin, jnp.zeros_like(sin), pad], axis=1)
    rep = LANES // DA_DK
    return jnp.tile(c64, (1, rep)), jnp.tile(s1_64, (1, rep)), jnp.tile(s2_64, (1, rep)), cos.T, sin.T


def kernel(x, c, positions, w_ada, b_ada, norm_mix, norm_ffn, w_in, diff_lambda, diff_subln, gmlp_ln_g, gmlp_ln_b, gmlp_w_spatial, gmlp_b_spatial, pool_w, pool_scale, conv_w, w_branch, w_gate, b_gate, w_out, router_w, router_b, expert_w_gu, expert_b_gu, expert_w_down, expert_b_down, final_norm):
    bsz, seq, d = x.shape
    depth = w_ada.shape[0]
    t = bsz * seq
    assert seq % 512 == 0 and d == 2 * MIX_W and d // 2 == SUBLANES * LANES

    rot_c, rot_s1, rot_s2, cos_t, sin_t = _rotary_tables(positions)
    c_pad = jnp.zeros((SUBLANES, d), F32).at[:bsz].set(c)
    mod = _adaln(c_pad, w_ada, b_ada)[:, :bsz].reshape(depth, bsz, 6, 1, d)
    rw_pad = jnp.zeros((depth, d, LANES), F32).at[:, :, :N_EXPERTS].set(router_w)
    rb_pad = jnp.full((depth, 1, LANES), NEG_INF, F32).at[:, 0, :N_EXPERTS].set(router_b)

    w_in_b, w_gate_b, w_branch_b, w_out_b = (w.astype(BF16) for w in (w_in, w_gate, w_branch, w_out))

    xf = x.reshape(t, d)
    h = None
    for l in range(depth):
        lam_init = 0.8 - 0.6 * math.exp(-0.3 * l)
        sh1, sc1, g1, sh2, sc2, g2 = (mod[l, :, i] for i in range(6))
        if l == 0:
            h = _norm_mod(xf, norm_mix[l].reshape(1, d), sc1, sh1, seq)
        z, qt, vt = _inproj(h, w_in_b, l, rot_c, rot_s1, rot_s2, cos_t, sin_t)
        y_a, w_gu_b, w_dn_b = _attention(z, qt, vt, diff_lambda[l], diff_subln[l].reshape(DA_DV, 1),
                                         expert_w_gu, expert_w_down, l, bsz, seq, lam_init)
        bs_b = jnp.broadcast_to(gmlp_b_spatial[l][:, :, None], (GM_GROUPS, GM_CHUNK, GM_CHUNK))
        y_b = _gmlp(z, gmlp_ln_g[l].reshape(1, MIX_W), gmlp_ln_b[l].reshape(1, MIX_W),
                    gmlp_w_spatial[l], bs_b)
        y_c = _pool(z, pool_w[l].astype(BF16), pool_scale[l].reshape(1, MIX_W), seq)
        y_d = _conv(z, conv_w[l], seq)
        merged = _merge(h, (y_a, y_b, y_c, y_d), w_gate_b, b_gate[l], w_branch_b, l)
        xf, hp, ti, tw, tr, cnt = _outproj(merged, w_out_b, l, xf, g1, norm_ffn[l].reshape(1, d),
                                           sc2, sh2, rw_pad[l], rb_pad[l], seq)
        dest, block_e, n_used, block_valid, runs, n_rows, pad_lo, pad_hi = _routing_tables(
            ti[:, :TOP_K], tr[:, :TOP_K], cnt[0, :N_EXPERTS])
        xb = _dispatch(pad_lo, pad_hi, n_used, dest, hp, n_rows)
        outp = _experts(block_e, n_used, block_valid, *runs, xb, w_gu_b, expert_b_gu[l], w_dn_b,
                        expert_b_down[l])
        if l + 1 < depth:
            nsh1, nsc1 = mod[l + 1, :, 0], mod[l + 1, :, 1]
            xf, h = _combine(dest, tw, xf, g2, norm_mix[l + 1].reshape(1, d), nsc1, nsh1, outp, seq, False)
        else:
            zero = jnp.zeros((bsz, 1, d), F32)
            (out,) = _combine(dest, tw, xf, g2, final_norm.reshape(1, d), zero, zero, outp, seq, True)
    return out.reshape(bsz, seq, d)
```

```python
import functools
import math

import jax
import jax.numpy as jnp
from jax import lax
from jax.experimental import pallas as pl
from jax.experimental.pallas import tpu as pltpu

F32 = jnp.float32
BF16 = jnp.bfloat16
U32 = jnp.uint32
I32 = jnp.int32

MIX_W = 1024
DA_HEADS = 8
DA_DK = 64
DA_DV = 128
ROT_DIM = 16
ROPE_THETA = 500000.0
GM_GROUPS = 8
GM_CHUNK = 128
POOL_WINDOWS = (2, 4, 8, 16)
POOL_GC = 256
N_BRANCH = 4
N_EXPERTS = 32
TOP_K = 4
D_EXPERT = 1024
SWIGLU_LIMIT = 7.0
SWIGLU_ALPHA = 1.702
MOE_BLOCK = 512
EPS = 1e-6
NEG_INF = -1e30

LANES = 128
SUBLANES = 8
HALO = 16
ONES_ROWS = 16
VMEM_LIMIT = 56 * 1024 * 1024

COL_K, COL_GU, COL_GV, COL_POOL, COL_CB, COL_CC, COL_CH = range(7)
IN_Q, IN_K, IN_V = 0, 1, 2


def _params(*sem):
    return pltpu.CompilerParams(dimension_semantics=sem, vmem_limit_bytes=VMEM_LIMIT)


def _rms_mod(x, g, sc, sh):
    y = x * lax.rsqrt(jnp.mean(x * x, axis=-1, keepdims=True) + EPS) * g
    return y * (1.0 + sc) + sh


def _pack_pair(lo, hi):
    lo_bits = lax.bitcast_convert_type(lo.astype(BF16).astype(F32), U32) >> 16
    hi_bits = lax.bitcast_convert_type(hi.astype(BF16).astype(F32), U32) & jnp.uint32(0xFFFF0000)
    return hi_bits | lo_bits


def _unpack_pair(w):
    lo = lax.bitcast_convert_type(w << 16, F32)
    hi = lax.bitcast_convert_type(w & jnp.uint32(0xFFFF0000), F32)
    return lo, hi


def _store_row_tiles(ref, packed):
    n = packed.shape[0]
    for c in range(SUBLANES):
        ref[pl.ds(c, n, stride=SUBLANES), :] = packed[:, c * LANES:(c + 1) * LANES]


def _load_row_tile_col(ref, c, n):
    return ref[pl.ds(c, n, stride=SUBLANES), :]


def _adaln_kernel(c_ref, w_ref, b_ref, o_ref):
    c = c_ref[...]
    o_ref[...] = jnp.dot(c * jax.nn.sigmoid(c), w_ref[...], preferred_element_type=F32) + b_ref[...]


def _adaln(c_pad, w_ada, b_ada):
    depth, d, n = w_ada.shape
    bn = 1536
    rows = c_pad.shape[0]
    return pl.pallas_call(
        _adaln_kernel,
        grid=(depth, n // bn),
        in_specs=[pl.BlockSpec((rows, d), lambda l, j: (0, 0)),
                  pl.BlockSpec((None, d, bn), lambda l, j: (l, 0, j)),
                  pl.BlockSpec((None, 1, bn), lambda l, j: (l, 0, j))],
        out_specs=pl.BlockSpec((None, rows, bn), lambda l, j: (l, 0, j)),
        out_shape=jax.ShapeDtypeStruct((depth, rows, n), F32),
        compiler_params=_params("arbitrary", "arbitrary"),
        name="adaln",
    )(c_pad, w_ada, b_ada.reshape(depth, 1, n))


def _norm_kernel(x_ref, g_ref, sc_ref, sh_ref, h_ref):
    h_ref[...] = _rms_mod(x_ref[...], g_ref[...], sc_ref[...], sh_ref[...]).astype(BF16)


def _norm_mod(x, g, sc, sh, seq):
    t, d = x.shape
    bm = 512
    tps = seq // bm
    vec = pl.BlockSpec((None, 1, d), lambda i: (i // tps, 0, 0))
    return pl.pallas_call(
        _norm_kernel,
        grid=(t // bm,),
        in_specs=[pl.BlockSpec((bm, d), lambda i: (i, 0)),
                  pl.BlockSpec((1, d), lambda i: (0, 0)), vec, vec],
        out_specs=pl.BlockSpec((bm, d), lambda i: (i, 0)),
        out_shape=jax.ShapeDtypeStruct((t, d), BF16),
        compiler_params=_params("arbitrary"),
        name="norm_mod",
    )(x, g, sc, sh)


def _inproj_kernel(h_ref, w_ref, c_ref, s1_ref, s2_ref, ct_ref, st_ref, z_ref, qt_ref, vt_ref):
    j = pl.program_id(1)
    acc = jnp.dot(h_ref[...], w_ref[...], preferred_element_type=F32)
    n_tiles = acc.shape[1] // LANES
    half = ROT_DIM // 2
    z_ref[...] = acc.astype(BF16)

    @pl.when(j == IN_Q)
    def _():
        scale = DA_DK ** -0.5 * math.log2(math.e)
        cos = ct_ref[...] * scale
        sin = st_ref[...] * scale
        for t in range(n_tiles):
            xt = acc[:, t * LANES:(t + 1) * LANES].T
            parts = []
            for comp in range(LANES // DA_DK):
                base = comp * DA_DK
                x1 = xt[base:base + half]
                x2 = xt[base + half:base + ROT_DIM]
                parts += [x1 * cos - x2 * sin, x2 * cos + x1 * sin, xt[base + ROT_DIM:base + DA_DK] * scale]
            qt_ref[t * LANES:(t + 1) * LANES, :] = jnp.concatenate(parts, axis=0).astype(BF16)

    @pl.when(j == IN_K)
    def _():
        for t in range(n_tiles):
            xt = acc[:, t * LANES:(t + 1) * LANES]
            r = (xt * c_ref[...] + pltpu.roll(xt, half, 1) * s1_ref[...]
                 + pltpu.roll(xt, LANES - half, 1) * s2_ref[...])
            z_ref[:, t * LANES:(t + 1) * LANES] = r.astype(BF16)

    @pl.when(j == IN_V)
    def _():
        for t in range(n_tiles):
            vt_ref[t * LANES:(t + 1) * LANES, :] = acc[:, t * LANES:(t + 1) * LANES].T.astype(BF16)


def _inproj(h, w_in, layer, rot_c, rot_s1, rot_s2, cos_t, sin_t):
    t, d = h.shape
    n = w_in.shape[2]
    bm, bn = 1024, MIX_W
    nz = n // bn - 2
    rot = pl.BlockSpec((bm, LANES), lambda i, j: (i, 0))
    rot_t = pl.BlockSpec((ROT_DIM // 2, bm), lambda i, j: (0, i))
    z_col = lambda j: jnp.where(j <= IN_K, 0, jnp.maximum(j - 2, 1))
    return pl.pallas_call(
        _inproj_kernel,
        grid=(t // bm, n // bn),
        in_specs=[pl.BlockSpec((bm, d), lambda i, j: (i, 0)),
                  pl.BlockSpec((None, d, bn), lambda i, j: (layer, 0, j)), rot, rot, rot, rot_t, rot_t],
        out_specs=[pl.BlockSpec((bm, bn), lambda i, j: (i, z_col(j))),
                   pl.BlockSpec((bn, bm), lambda i, j: (0, i)),
                   pl.BlockSpec((bn, bm), lambda i, j: (0, i))],
        out_shape=[jax.ShapeDtypeStruct((t, nz * bn), BF16),
                   jax.ShapeDtypeStruct((bn, t), BF16),
                   jax.ShapeDtypeStruct((bn, t), BF16)],
        compiler_params=_params("arbitrary", "arbitrary"),
        name="inproj",
    )(h, w_in, rot_c, rot_s1, rot_s2, cos_t, sin_t)


def _attn_kernel(qt_ref, k_ref, vt_ref, lam_ref, sub_ref, wgu_ref, wdn_ref, o_ref, wgu_o_ref, wdn_o_ref,
                 sa_ref, sb_ref, ma_sc, mb_sc, m_sc, acc_sc, *, tq, lam_init):
    i = pl.program_id(2)
    wgu_o_ref[...] = wgu_ref[...].astype(BF16)
    wdn_o_ref[...] = wdn_ref[...].astype(BF16)
    qt = qt_ref[...]
    row = lax.broadcasted_iota(I32, qt.shape, 0)
    zero = jnp.zeros_like(qt)
    qs = jnp.concatenate([jnp.where(row < DA_DK, qt, zero), jnp.where(row >= DA_DK, qt, zero)], axis=1)
    m_sc[...] = jnp.full(m_sc.shape, NEG_INF, F32)
    acc_sc[...] = jnp.zeros(acc_sc.shape, F32)

    def scores(j, s_ref, smax_ref):
        off = pl.multiple_of(j * tq, tq)
        s = jnp.dot(k_ref[pl.ds(off, tq), :], qs, preferred_element_type=F32)
        s_ref[...] = s
        smax_ref[...] = jnp.max(s, axis=0, keepdims=True)

    def softmax_pv(j, s_ref, smax_ref, masked):
        off = pl.multiple_of(j * tq, tq)
        s = s_ref[...]
        if masked:
            key = lax.broadcasted_iota(I32, s.shape, 0)
            c2 = lax.broadcasted_iota(I32, s.shape, 1)
            s = jnp.where(key <= jnp.where(c2 >= tq, c2 - tq, c2), s, NEG_INF)
            s_max = jnp.max(s, axis=0, keepdims=True)
        else:
            s_max = smax_ref[...]
        m_prev = m_sc[...]
        m_new = jnp.maximum(m_prev, s_max)
        alpha = jnp.exp2(m_prev - m_new)
        p = jnp.exp2(s - m_new).astype(BF16)
        vt_aug = jnp.concatenate([vt_ref[:, pl.ds(off, tq)], jnp.ones((ONES_ROWS, tq), BF16)], axis=0)
        acc_sc[...] = alpha * acc_sc[...] + jnp.dot(vt_aug, p, preferred_element_type=F32)
        m_sc[...] = m_new

    scores(0, sa_ref, ma_sc)

    def pair(j):
        scores(j + 1, sb_ref, mb_sc)
        softmax_pv(j, sa_ref, ma_sc, False)
        scores(j + 2, sa_ref, ma_sc)
        softmax_pv(j + 1, sb_ref, mb_sc, False)

    def octet(jj, carry):
        for q in range(4):
            pair(8 * jj + 2 * q)
        return carry

    n_oct = i // 8
    lax.fori_loop(0, n_oct, octet, 0)
    done = 8 * n_oct
    rest = i - done

    @pl.when(rest >= 4)
    def _():
        pair(done)
        pair(done + 2)

    done = done + jnp.where(rest >= 4, 4, 0)

    @pl.when(i - done >= 2)
    def _():
        pair(done)

    @pl.when(i % 2 == 1)
    def _():
        scores(i, sb_ref, mb_sc)
        softmax_pv(i - 1, sa_ref, ma_sc, False)
        softmax_pv(i, sb_ref, mb_sc, True)

    @pl.when(i % 2 == 0)
    def _():
        softmax_pv(i, sa_ref, ma_sc, True)

    lv = lam_ref[...]
    lam = (jnp.exp(jnp.sum(lv[0:1] * lv[1:2], axis=-1, keepdims=True))
           - jnp.exp(jnp.sum(lv[2:3] * lv[3:4], axis=-1, keepdims=True)) + lam_init)
    acc = acc_sc[:DA_DV, :]
    l = acc_sc[DA_DV:DA_DV + 1, :]
    o = acc[:, :tq] / l[:, :tq] - lam * (acc[:, tq:] / l[:, tq:])
    y = o * lax.rsqrt(jnp.mean(o * o, axis=0, keepdims=True) + EPS) * sub_ref[...] * (1.0 - lam_init)
    o_ref[...] = y.T.astype(o_ref.dtype)


def _attention(z, qt, vt, lam_vecs, subln_col, w_gu, w_dn, layer, bsz, seq, lam_init):
    t = z.shape[0]
    tq = min(512, seq)
    nq = seq // tq
    steps = bsz * DA_HEADS * nq
    _, ne, d, gu_w = w_gu.shape
    gu_rows, dn_rows = ne * d // steps, ne * D_EXPERT // steps
    assert gu_rows * steps == ne * d and dn_rows * steps == ne * D_EXPERT and dn_rows % HALO == 0
    step = lambda b, h, i: (b * DA_HEADS + h) * nq + i
    y_a, gu_b, dn_b = pl.pallas_call(
        functools.partial(_attn_kernel, tq=tq, lam_init=lam_init),
        grid=(bsz, DA_HEADS, nq),
        in_specs=[pl.BlockSpec((DA_DV, tq), lambda b, h, i: (h, b * nq + i)),
                  pl.BlockSpec((seq, DA_DV), lambda b, h, i: (b, COL_K * DA_HEADS + h)),
                  pl.BlockSpec((DA_DV, seq), lambda b, h, i: (h, b)),
                  pl.BlockSpec((4, DA_DK), lambda b, h, i: (0, 0)),
                  pl.BlockSpec((DA_DV, 1), lambda b, h, i: (0, 0)),
                  pl.BlockSpec((None, gu_rows, gu_w), lambda b, h, i: (layer, step(b, h, i), 0)),
                  pl.BlockSpec((None, dn_rows, d), lambda b, h, i: (layer, step(b, h, i), 0))],
        out_specs=[pl.BlockSpec((tq, DA_DV), lambda b, h, i: (b * nq + i, h)),
                   pl.BlockSpec((gu_rows, gu_w), lambda b, h, i: (step(b, h, i), 0)),
                   pl.BlockSpec((dn_rows, d), lambda b, h, i: (step(b, h, i), 0))],
        out_shape=[jax.ShapeDtypeStruct((t, MIX_W), BF16),
                   jax.ShapeDtypeStruct((ne * d, gu_w), BF16),
                   jax.ShapeDtypeStruct((ne * D_EXPERT, d), BF16)],
        scratch_shapes=[pltpu.VMEM((tq, 2 * tq), F32), pltpu.VMEM((tq, 2 * tq), F32),
                        pltpu.VMEM((1, 2 * tq), F32), pltpu.VMEM((1, 2 * tq), F32),
                        pltpu.VMEM((1, 2 * tq), F32), pltpu.VMEM((DA_DV + ONES_ROWS, 2 * tq), F32)],
        compiler_params=_params("arbitrary", "arbitrary", "arbitrary"),
        name="diff_attn",
    )(qt, z, vt, lam_vecs, subln_col, w_gu.reshape(w_gu.shape[0], ne * d, gu_w),
      w_dn.reshape(w_dn.shape[0], ne * D_EXPERT, d))
    return y_a, gu_b.reshape(ne, d, gu_w), dn_b.reshape(ne, D_EXPERT, d)


def _gelu(x):
    return 0.5 * x * (1.0 + lax.erf(x * math.sqrt(0.5)))


def _gmlp_kernel(u_ref, v_ref, g_ref, b_ref, ws_ref, bs_ref, o_ref, *, bm):
    v = _gelu(v_ref[...].astype(F32))
    mu = jnp.mean(v, axis=-1, keepdims=True)
    vc = v - mu
    var = jnp.mean(vc * vc, axis=-1, keepdims=True)
    vn = (vc * lax.rsqrt(var + EPS) * g_ref[...] + b_ref[...]).astype(BF16)
    r = lax.broadcasted_iota(I32, (GM_CHUNK, GM_CHUNK), 0)
    c = lax.broadcasted_iota(I32, (GM_CHUNK, GM_CHUNK), 1)
    for g in range(GM_GROUPS):
        cols = slice(g * GM_CHUNK, (g + 1) * GM_CHUNK)
        wm = jnp.where(c <= r, ws_ref[g], 0.0).astype(BF16)
        bias = bs_ref[g]
        for n in range(bm // GM_CHUNK):
            rows = slice(n * GM_CHUNK, (n + 1) * GM_CHUNK)
            sv = jnp.dot(wm, vn[rows, cols], preferred_element_type=F32) + bias
            u = _gelu(u_ref[rows, cols].astype(F32))
            o_ref[rows, cols] = (u * sv).astype(o_ref.dtype)


def _gmlp(z, ln_g, ln_b, ws, bs_b):
    t = z.shape[0]
    bm = 512
    full = lambda shape: pl.BlockSpec(shape, lambda i: (0,) * len(shape))
    return pl.pallas_call(
        functools.partial(_gmlp_kernel, bm=bm),
        grid=(t // bm,),
        in_specs=[pl.BlockSpec((bm, MIX_W), lambda i: (i, COL_GU)),
                  pl.BlockSpec((bm, MIX_W), lambda i: (i, COL_GV)),
                  full((1, MIX_W)), full((1, MIX_W)),
                  full((GM_GROUPS, GM_CHUNK, GM_CHUNK)), full((GM_GROUPS, GM_CHUNK, GM_CHUNK))],
        out_specs=pl.BlockSpec((bm, MIX_W), lambda i: (i, 0)),
        out_shape=jax.ShapeDtypeStruct((t, MIX_W), BF16),
        compiler_params=_params("arbitrary"),
        name="gmlp",
    )(z, z, ln_g, ln_b, ws, bs_b)


def _pool_kernel(z_ref, halo_ref, pw_ref, ps_ref, o_ref, band_sc, bandh_sc, *, bm, tps):
    p0 = (pl.program_id(0) % tps) * bm

    @pl.when(pl.program_id(0) == 0)
    def _():
        d = (lax.broadcasted_iota(I32, (bm, bm), 0) - lax.broadcasted_iota(I32, (bm, bm), 1))
        dh = (lax.broadcasted_iota(I32, (bm, HALO), 0) + HALO - lax.broadcasted_iota(I32, (bm, HALO), 1))
        for g, w in enumerate(POOL_WINDOWS):
            band_sc[g] = jnp.where(d >= 0, jnp.where(d < w, 1.0, 0.0), 0.0).astype(BF16)
            bandh_sc[g] = jnp.where(dh < w, 1.0, 0.0).astype(BF16)

    pos = p0 + lax.broadcasted_iota(I32, (bm, 1), 0)
    for g, w in enumerate(POOL_WINDOWS):
        cols = slice(g * POOL_GC, (g + 1) * POOL_GC)
        x = z_ref[:, cols]
        hx = halo_ref[:, cols]
        hx = jnp.where(p0 > 0, hx, jnp.zeros_like(hx))
        win = (jnp.dot(band_sc[g], x, preferred_element_type=F32)
               + jnp.dot(bandh_sc[g], hx, preferred_element_type=F32))
        cnt = jnp.minimum(pos + 1, w).astype(F32)
        pooled = (win / cnt - x.astype(F32)).astype(BF16)
        mixed = jnp.dot(pooled, pw_ref[g], preferred_element_type=F32)
        o_ref[:, cols] = (mixed * ps_ref[:, cols]).astype(o_ref.dtype)


def _halo_spec(bm, col):
    return pl.BlockSpec((HALO, MIX_W), lambda i: (jnp.maximum(i * (bm // HALO) - 1, 0), col))


def _pool(z, pw, ps, seq):
    t = z.shape[0]
    bm = min(512, seq)
    ng = len(POOL_WINDOWS)
    return pl.pallas_call(
        functools.partial(_pool_kernel, bm=bm, tps=seq // bm),
        grid=(t // bm,),
        in_specs=[pl.BlockSpec((bm, MIX_W), lambda i: (i, COL_POOL)),
                  _halo_spec(bm, COL_POOL),
                  pl.BlockSpec((ng, POOL_GC, POOL_GC), lambda i: (0, 0, 0)),
                  pl.BlockSpec((1, MIX_W), lambda i: (0, 0))],
        out_specs=pl.BlockSpec((bm, MIX_W), lambda i: (i, 0)),
        out_shape=jax.ShapeDtypeStruct((t, MIX_W), BF16),
        scratch_shapes=[pltpu.VMEM((ng, bm, bm), BF16), pltpu.VMEM((ng, bm, HALO), BF16)],
        compiler_params=_params("arbitrary"),
        name="pool",
    )(z, z, pw, ps)


def _conv_kernel(b_ref, c_ref, h_ref, hc_ref, hh_ref, cw_ref, o_ref, *, tps):
    first = (pl.program_id(0) % tps) == 0
    xin = c_ref[...].astype(F32) * h_ref[...].astype(F32)
    hal = hc_ref[...].astype(F32) * hh_ref[...].astype(F32)
    hal = jnp.where(first, 0.0, hal)
    hm1 = hal[HALO - 1:HALO, :]
    hm2 = hal[HALO - 2:HALO - 1, :]
    row = lax.broadcasted_iota(I32, xin.shape, 0)
    s1 = jnp.where(row == 0, hm1, pltpu.roll(xin, 1, 0))
    s2 = jnp.where(row == 0, hm2, jnp.where(row == 1, hm1, pltpu.roll(xin, 2, 0)))
    cw = cw_ref[...]
    y = cw[0:1] * s2 + cw[1:2] * s1 + cw[2:3] * xin
    o_ref[...] = (b_ref[...].astype(F32) * y).astype(o_ref.dtype)


def _conv(z, cw, seq):
    t = z.shape[0]
    bm = min(512, seq)
    blk = lambda col: pl.BlockSpec((bm, MIX_W), lambda i: (i, col))
    return pl.pallas_call(
        functools.partial(_conv_kernel, tps=seq // bm),
        grid=(t // bm,),
        in_specs=[blk(COL_CB), blk(COL_CC), blk(COL_CH), _halo_spec(bm, COL_CC), _halo_spec(bm, COL_CH),
                  pl.BlockSpec(cw.shape, lambda i: (0, 0))],
        out_specs=pl.BlockSpec((bm, MIX_W), lambda i: (i, 0)),
        out_shape=jax.ShapeDtypeStruct((t, MIX_W), BF16),
        compiler_params=_params("arbitrary"),
        name="short_conv",
    )(z, z, z, z, z, cw)


def _merge_kernel(h_ref, ya_ref, yb_ref, yc_ref, yd_ref, wg_ref, bg_ref, wb_ref, o_ref):
    h = h_ref[...]
    bg = bg_ref[...]
    merged = None
    for i, y_ref in enumerate((ya_ref, yb_ref, yc_ref, yd_ref)):
        gate = jax.nn.sigmoid(jnp.dot(h, wg_ref[i], preferred_element_type=F32) + bg[i:i + 1])
        term = gate * jnp.dot(y_ref[...], wb_ref[i], preferred_element_type=F32)
        merged = term if merged is None else merged + term
    o_ref[...] = merged.astype(o_ref.dtype)


def _merge(h, ys, wg, bg, wb, layer):
    t, d = h.shape
    bm, bn = 1024, 256
    yspec = pl.BlockSpec((bm, MIX_W), lambda i, j: (i, 0))
    return pl.pallas_call(
        _merge_kernel,
        grid=(t // bm, d // bn),
        in_specs=[pl.BlockSpec((bm, d), lambda i, j: (i, 0)), yspec, yspec, yspec, yspec,
                  pl.BlockSpec((None, N_BRANCH, d, bn), lambda i, j: (layer, 0, 0, j)),
                  pl.BlockSpec((N_BRANCH, bn), lambda i, j: (0, j)),
                  pl.BlockSpec((None, N_BRANCH, MIX_W, bn), lambda i, j: (layer, 0, 0, j))],
        out_specs=pl.BlockSpec((bm, bn), lambda i, j: (i, j)),
        out_shape=jax.ShapeDtypeStruct((t, d), BF16),
        compiler_params=_params("arbitrary", "arbitrary"),
        name="branch_merge",
    )(h, *ys, wg, bg, wb)


def _outproj_kernel(m_ref, wo_ref, x_ref, g1_ref, ng_ref, sc_ref, sh_ref, rw_ref, rb_ref,
                    xo_ref, hp_ref, ti_ref, tw_ref, tr_ref, cnt_ref, carry_sc):
    out = jnp.dot(m_ref[...], wo_ref[...], preferred_element_type=F32)
    x_new = x_ref[...] + g1_ref[...] * out
    xo_ref[...] = x_new
    h2 = _rms_mod(x_new, ng_ref[...], sc_ref[...], sh_ref[...])
    half = h2.shape[1] // 2
    _store_row_tiles(hp_ref, _pack_pair(h2[:, :half], h2[:, half:]))

    logits = jnp.dot(h2, rw_ref[...], preferred_element_type=F32) + rb_ref[...]
    lane = lax.broadcasted_iota(I32, logits.shape, 1).astype(F32)
    vals, idxs = [], []
    cur = logits
    for _ in range(TOP_K):
        m = jnp.max(cur, axis=-1, keepdims=True)
        idx = jnp.min(jnp.where(cur == m, lane, float(LANES)), axis=-1, keepdims=True)
        vals.append(m)
        idxs.append(idx)
        cur = jnp.where(lane == idx, -3e38, cur)
    es = [jnp.exp(v - vals[0]) for v in vals]
    den = es[0] + es[1] + es[2] + es[3]

    @pl.when(pl.program_id(0) == 0)
    def _():
        carry_sc[...] = jnp.zeros(carry_sc.shape, F32)

    onehot = jnp.zeros(logits.shape, F32)
    for k in range(TOP_K):
        onehot = onehot + jnp.where(lane == idxs[k], 1.0, 0.0)
    bm = logits.shape[0]
    tri = jnp.where(lax.broadcasted_iota(I32, (bm, bm), 0) > lax.broadcasted_iota(I32, (bm, bm), 1),
                    1.0, 0.0).astype(BF16)
    before = jnp.dot(tri, onehot.astype(BF16), preferred_element_type=F32) + carry_sc[...]
    carry = carry_sc[...] + jnp.sum(onehot, axis=0, keepdims=True)
    carry_sc[...] = carry
    cnt_ref[...] = jnp.broadcast_to(carry, cnt_ref.shape)

    ti = jnp.zeros(logits.shape, F32)
    tw = jnp.zeros(logits.shape, F32)
    tr = jnp.zeros(logits.shape, F32)
    for k in range(TOP_K):
        rank = jnp.sum(jnp.where(lane == idxs[k], before, 0.0), axis=-1, keepdims=True)
        ti = jnp.where(lane == float(k), idxs[k], ti)
        tw = jnp.where(lane == float(k), es[k] / den, tw)
        tr = jnp.where(lane == float(k), rank, tr)
    ti_ref[...] = ti.astype(I32)
    tw_ref[...] = tw
    tr_ref[...] = tr.astype(I32)


def _outproj(merged, wo, layer, x, g1, ng, sc, sh, rw, rb, seq):
    t, d = x.shape
    bm = 512
    tps = seq // bm
    row = lambda w: pl.BlockSpec((bm, w), lambda i: (i, 0))
    vec = pl.BlockSpec((None, 1, d), lambda i: (i // tps, 0, 0))
    full = lambda a: pl.BlockSpec(a.shape, lambda i: (0, 0))
    return pl.pallas_call(
        _outproj_kernel,
        grid=(t // bm,),
        in_specs=[row(d), pl.BlockSpec((None, d, d), lambda i: (layer, 0, 0)), row(d), vec, full(ng), vec, vec,
                  full(rw), full(rb)],
        out_specs=[row(d), pl.BlockSpec((bm * SUBLANES, LANES), lambda i: (i, 0)),
                   row(LANES), row(LANES), row(LANES),
                   pl.BlockSpec((SUBLANES, LANES), lambda i: (0, 0))],
        out_shape=[jax.ShapeDtypeStruct((t, d), F32), jax.ShapeDtypeStruct((t * SUBLANES, LANES), U32),
                   jax.ShapeDtypeStruct((t, LANES), I32), jax.ShapeDtypeStruct((t, LANES), F32),
                   jax.ShapeDtypeStruct((t, LANES), I32), jax.ShapeDtypeStruct((SUBLANES, LANES), F32)],
        scratch_shapes=[pltpu.VMEM((1, LANES), F32)],
        compiler_params=_params("arbitrary"),
        name="outproj_router",
    )(merged, wo, x, g1, ng, sc, sh, rw, rb)


def _row_copy(src_ref, src_row, dst_ref, dst_row, sem):
    return pltpu.make_async_copy(src_ref.at[pl.ds(pl.multiple_of(src_row * SUBLANES, SUBLANES), SUBLANES), :],
                                 dst_ref.at[pl.ds(pl.multiple_of(dst_row * SUBLANES, SUBLANES), SUBLANES), :],
                                 sem)


def _dispatch_kernel(pad_lo_ref, pad_hi_ref, nu_ref, dest_ref, hp_ref, xb_hbm, zeros_sc, sem, zsem,
                     *, bm, n_blk):
    blk_rows = MOE_BLOCK * SUBLANES

    @pl.when(pl.program_id(0) == 0)
    def _():
        zeros_sc[...] = jnp.zeros(zeros_sc.shape, zeros_sc.dtype)

        def pad_rows(act):
            def per_expert(e, carry):
                row = pad_lo_ref[e]
                n_pad = pad_hi_ref[e] - row
                p = MOE_BLOCK // 2
                while p >= 1:
                    take = (n_pad & p) != 0

                    @pl.when(take)
                    def _(row=row, p=p):
                        dst = xb_hbm.at[pl.ds(pl.multiple_of(row * SUBLANES, SUBLANES), p * SUBLANES), :]
                        act(pltpu.make_async_copy(zeros_sc.at[pl.ds(0, p * SUBLANES), :], dst, zsem))

                    row = row + jnp.where(take, p, 0)
                    p //= 2
                return carry
            lax.fori_loop(0, N_EXPERTS, per_expert, 0)

        def tail_blocks(act):
            def per_block(b, carry):
                off = pl.multiple_of(b * blk_rows, blk_rows)
                act(pltpu.make_async_copy(zeros_sc, xb_hbm.at[pl.ds(off, blk_rows), :], zsem))
                return carry
            lax.fori_loop(nu_ref[0], n_blk, per_block, 0)

        pad_rows(lambda cp: cp.start())
        tail_blocks(lambda cp: cp.start())
        pad_rows(lambda cp: cp.wait())
        tail_blocks(lambda cp: cp.wait())

    def issue(r, carry):
        for k in range(TOP_K):
            _row_copy(hp_ref, r, xb_hbm, dest_ref[0, 0, r * TOP_K + k], sem).start(priority=k % 2)
        return carry

    lax.fori_loop(0, bm, issue, 0, unroll=4)
    for k in range(TOP_K):
        pltpu.make_async_copy(hp_ref, xb_hbm.at[pl.ds(0, bm * SUBLANES), :], sem).wait()


def _dispatch(pad_lo, pad_hi, n_used, dest, hp, n_rows):
    t = dest.shape[0]
    bm = 512
    nt = t // bm
    n_blk = n_rows // MOE_BLOCK
    return pl.pallas_call(
        functools.partial(_dispatch_kernel, bm=bm, n_blk=n_blk),
        grid_spec=pltpu.PrefetchScalarGridSpec(
            num_scalar_prefetch=3,
            grid=(nt,),
            in_specs=[pl.BlockSpec((1, 1, bm * TOP_K), lambda i, *_: (i, 0, 0), memory_space=pltpu.SMEM),
                      pl.BlockSpec((bm * SUBLANES, LANES), lambda i, *_: (i, 0))],
            out_specs=pl.BlockSpec(memory_space=pl.ANY),
            scratch_shapes=[pltpu.VMEM((MOE_BLOCK * SUBLANES, LANES), U32),
                            pltpu.SemaphoreType.DMA(()), pltpu.SemaphoreType.DMA(())]),
        out_shape=jax.ShapeDtypeStruct((n_rows * SUBLANES, LANES), U32),
        compiler_params=_params("arbitrary"),
        name="moe_dispatch",
    )(pad_lo, pad_hi, n_used, dest.reshape(nt, 1, bm * TOP_K), hp)


def _expert_kernel(be_ref, nu_ref, bv_ref, first_ref, slot_ref, nxt_ref, xb_ref, wgu_hbm, bgu_ref, wdn_hbm,
                   bdn_ref, o_ref, wgu_buf, wdn_buf, sem):
    b = pl.program_id(0)
    valid = bv_ref[b]
    slot = slot_ref[b]

    def weight_copies(e, s):
        return (pltpu.make_async_copy(wgu_hbm.at[e], wgu_buf.at[s], sem.at[0, s]),
                pltpu.make_async_copy(wdn_hbm.at[e], wdn_buf.at[s], sem.at[1, s]))

    @pl.when(b == 0)
    def _():
        for cp in weight_copies(be_ref[0], 0):
            cp.start()

    @pl.when(first_ref[b] == 1)
    def _():
        @pl.when(nxt_ref[b] >= 0)
        def _():
            for cp in weight_copies(nxt_ref[b], 1 - slot):
                cp.start(priority=1)

        for cp in weight_copies(be_ref[b], slot):
            cp.wait()

    wgu_ref = wgu_buf.at[slot]
    wdn_ref = wdn_buf.at[slot]

    def ffn(rows):
        los, his = [], []
        for c in range(SUBLANES):
            lo, hi = _unpack_pair(_load_row_tile_col(xb_ref, c, rows))
            los.append(lo.astype(BF16))
            his.append(hi.astype(BF16))
        lo = jnp.concatenate(los, axis=1)
        hi = jnp.concatenate(his, axis=1)
        half = lo.shape[1]
        gu = (jnp.dot(lo, wgu_ref[:half, :], preferred_element_type=F32)
              + jnp.dot(hi, wgu_ref[half:, :], preferred_element_type=F32) + bgu_ref[...])
        g = jnp.minimum(gu[:, :D_EXPERT], SWIGLU_LIMIT)
        u = jnp.clip(gu[:, D_EXPERT:], -SWIGLU_LIMIT, SWIGLU_LIMIT)
        act = (u + 1.0) * (g * jax.nn.sigmoid(SWIGLU_ALPHA * g))
        out = jnp.dot(act.astype(BF16), wdn_ref[...], preferred_element_type=F32) + bdn_ref[...]
        _store_row_tiles(o_ref, _pack_pair(out[:, :half], out[:, half:]))
        if rows < MOE_BLOCK:
            o_ref[rows * SUBLANES:, :] = jnp.zeros(((MOE_BLOCK - rows) * SUBLANES, LANES), o_ref.dtype)

    @pl.when(valid > MOE_BLOCK // 2)
    def _():
        ffn(MOE_BLOCK)

    @pl.when(jnp.logical_and(valid > 0, valid <= MOE_BLOCK // 2))
    def _():
        ffn(MOE_BLOCK // 2)

    @pl.when(valid == 0)
    def _():
        o_ref[...] = jnp.zeros(o_ref.shape, o_ref.dtype)


def _experts(block_e, n_used, block_valid, run_first, run_slot, run_next, xb, w_gu, b_gu, w_dn, b_dn):
    n_blk = xb.shape[0] // (MOE_BLOCK * SUBLANES)
    ne, d, _ = w_gu.shape
    blk = pl.BlockSpec((MOE_BLOCK * SUBLANES, LANES), lambda b, be, *_: (b, 0))
    blk_in = pl.BlockSpec((MOE_BLOCK * SUBLANES, LANES), lambda b, be, nu, *_: (jnp.minimum(b, nu[0] - 1), 0))
    return pl.pallas_call(
        _expert_kernel,
        grid_spec=pltpu.PrefetchScalarGridSpec(
            num_scalar_prefetch=6,
            grid=(n_blk,),
            in_specs=[blk_in,
                      pl.BlockSpec(memory_space=pl.ANY),
                      pl.BlockSpec((None, 1, 2 * D_EXPERT), lambda b, be, *_: (be[b], 0, 0)),
                      pl.BlockSpec(memory_space=pl.ANY),
                      pl.BlockSpec((None, 1, d), lambda b, be, *_: (be[b], 0, 0))],
            out_specs=blk,
            scratch_shapes=[pltpu.VMEM((2, d, 2 * D_EXPERT), BF16), pltpu.VMEM((2, D_EXPERT, d), BF16),
                            pltpu.SemaphoreType.DMA((2, 2))]),
        out_shape=jax.ShapeDtypeStruct(xb.shape, U32),
        compiler_params=_params("arbitrary"),
        name="moe_experts",
    )(block_e, n_used, block_valid, run_first, run_slot, run_next, xb, w_gu, b_gu.reshape(ne, 1, -1), w_dn,
      b_dn.reshape(ne, 1, -1))


def _combine_kernel(dest_ref, dnext_ref, tw_ref, x_ref, g2_ref, ng_ref, sc_ref, sh_ref, outp_hbm, *rest,
                    bm, final):
    if final:
        ho_ref, buf, sem = rest
    else:
        xo_ref, ho_ref, buf, sem = rest
    i = pl.program_id(0)
    slot = i % 2

    def gather_tile(d_ref, s):
        def issue(r, carry):
            for k in range(TOP_K):
                _row_copy(outp_hbm, d_ref[0, 0, r * TOP_K + k], buf.at[s, k], r, sem.at[s]).start(priority=k % 2)
            return carry

        lax.fori_loop(0, bm, issue, 0, unroll=4)

    @pl.when(i == 0)
    def _():
        gather_tile(dest_ref, 0)

    @pl.when(i + 1 < pl.num_programs(0))
    def _():
        gather_tile(dnext_ref, 1 - slot)

    for k in range(TOP_K):
        pltpu.make_async_copy(outp_hbm.at[pl.ds(0, bm * SUBLANES), :], buf.at[slot, k], sem.at[slot]).wait()

    tw = tw_ref[...]
    wk = [tw[:, k:k + 1] for k in range(TOP_K)]
    y_lo, y_hi = [], []
    for c in range(SUBLANES):
        a_lo = a_hi = None
        for k in range(TOP_K):
            lo, hi = _unpack_pair(buf[slot, k, pl.ds(c, bm, stride=SUBLANES), :])
            a_lo = wk[k] * lo if a_lo is None else a_lo + wk[k] * lo
            a_hi = wk[k] * hi if a_hi is None else a_hi + wk[k] * hi
        y_lo.append(a_lo)
        y_hi.append(a_hi)
    x_new = x_ref[...] + g2_ref[...] * jnp.concatenate(y_lo + y_hi, axis=1)
    if final:
        ho_ref[...] = x_new * lax.rsqrt(jnp.mean(x_new * x_new, axis=-1, keepdims=True) + EPS) * ng_ref[...]
    else:
        xo_ref[...] = x_new
        ho_ref[...] = _rms_mod(x_new, ng_ref[...], sc_ref[...], sh_ref[...]).astype(BF16)


def _combine(dest, tw, x, g2, ng, sc, sh, outp, seq, final):
    t, d = x.shape
    bm = 256
    tps = seq // bm
    nt = t // bm
    row = lambda w: pl.BlockSpec((bm, w), lambda i: (i, 0))
    vec = pl.BlockSpec((None, 1, d), lambda i: (i // tps, 0, 0))
    dest3 = dest.reshape(nt, 1, bm * TOP_K)
    if final:
        out_specs = [row(d)]
        out_shape = [jax.ShapeDtypeStruct((t, d), F32)]
    else:
        out_specs = [row(d), row(d)]
        out_shape = [jax.ShapeDtypeStruct((t, d), F32), jax.ShapeDtypeStruct((t, d), BF16)]
    return pl.pallas_call(
        functools.partial(_combine_kernel, bm=bm, final=final),
        grid=(nt,),
        in_specs=[pl.BlockSpec((1, 1, bm * TOP_K), lambda i: (i, 0, 0), memory_space=pltpu.SMEM),
                  pl.BlockSpec((1, 1, bm * TOP_K), lambda i: (jnp.minimum(i + 1, nt - 1), 0, 0),
                               memory_space=pltpu.SMEM),
                  row(LANES), row(d), vec, pl.BlockSpec((1, d), lambda i: (0, 0)), vec, vec,
                  pl.BlockSpec(memory_space=pl.ANY)],
        out_specs=out_specs,
        out_shape=out_shape,
        scratch_shapes=[pltpu.VMEM((2, TOP_K, bm * SUBLANES, LANES), U32), pltpu.SemaphoreType.DMA((2,))],
        compiler_params=_params("arbitrary"),
        name="moe_combine",
    )(dest3, dest3, tw, x, g2, ng, sc, sh, outp)


def _routing_tables(top_i, rank, counts_f):
    t = top_i.shape[0]
    n_asg = t * TOP_K
    counts = counts_f.astype(I32)
    padded = ((counts + MOE_BLOCK - 1) // MOE_BLOCK) * MOE_BLOCK
    pend = jnp.cumsum(padded)
    pstart = pend - padded
    experts = jnp.arange(N_EXPERTS, dtype=I32)
    start_of = jnp.sum(jnp.where(top_i[:, :, None] == experts, pstart, 0), axis=-1)
    dest = (start_of + rank).astype(I32)
    n_rows = ((n_asg + N_EXPERTS * MOE_BLOCK + MOE_BLOCK - 1) // MOE_BLOCK) * MOE_BLOCK
    n_blk = n_rows // MOE_BLOCK
    first_row = jnp.arange(n_blk, dtype=I32) * MOE_BLOCK
    block_e = jnp.minimum(jnp.sum((pend[None, :] <= first_row[:, None]).astype(I32), axis=1), N_EXPERTS - 1)
    n_used = (pend[-1:] // MOE_BLOCK).astype(I32)
    pad_lo = (pstart + counts).astype(I32)
    seg_end = jnp.sum(jnp.where(block_e[:, None] == experts, pad_lo, 0), axis=-1)
    block_valid = jnp.clip(seg_end - first_row, 0, MOE_BLOCK).astype(I32)
    blk = jnp.arange(n_blk, dtype=I32)
    used = blk < n_used[0]
    prev_e = jnp.concatenate([jnp.full((1,), -1, I32), block_e[:-1]])
    run_first = jnp.logical_and(block_e != prev_e, used).astype(I32)
    run_slot = ((jnp.cumsum(run_first) - 1) % 2).astype(I32)
    next_blk = jnp.sum(jnp.where(block_e[:, None] == experts, pend, 0), axis=-1) // MOE_BLOCK
    next_e = jnp.sum(jnp.where(next_blk[:, None] == blk[None, :], block_e[None, :], 0), axis=-1)
    run_next = jnp.where(next_blk < n_used[0], next_e, -1).astype(I32)
    return dest, block_e, n_used, block_valid, (run_first, run_slot, run_next), n_rows, pad_lo, pend.astype(I32)


def _rotary_tables(positions):
    inv_freq = ROPE_THETA ** (-jnp.arange(0, ROT_DIM, 2, dtype=F32) / ROT_DIM)
    ang = positions.astype(F32).reshape(-1, 1) * inv_freq
    cos, sin = jnp.cos(ang), jnp.sin(ang)
    t = ang.shape[0]
    pad = jnp.zeros((t, DA_DK - ROT_DIM), F32)
    c64 = jnp.concatenate([cos, cos, pad + 1.0], axis=1)
    s1_64 = jnp.concatenate([jnp.zeros_like(sin), sin, pad], axis=1)
    s2_64 = jnp.concatenate([-sin, jnp.zeros_like(sin), pad], axis=1)
    rep = LANES // DA_DK
    return jnp.tile(c64, (1, rep)), jnp.tile(s1_64, (1, rep)), jnp.tile(s2_64, (1, rep)), cos.T, sin.T


def kernel(x, c, positions, w_ada, b_ada, norm_mix, norm_ffn, w_in, diff_lambda, diff_subln, gmlp_ln_g, gmlp_ln_b, gmlp_w_spatial, gmlp_b_spatial, pool_w, pool_scale, conv_w, w_branch, w_gate, b_gate, w_out, router_w, router_b, expert_w_gu, expert_b_gu, expert_w_down, expert_b_down, final_norm):
    bsz, seq, d = x.shape
    depth = w_ada.shape[0]
    t = bsz * seq
    assert seq % 512 == 0 and d == 2 * MIX_W and d // 2 == SUBLANES * LANES

    rot_c, rot_s1, rot_s2, cos_t, sin_t = _rotary_tables(positions)
    c_pad = jnp.zeros((SUBLANES, d), F32).at[:bsz].set(c)
    mod = _adaln(c_pad, w_ada, b_ada)[:, :bsz].reshape(depth, bsz, 6, 1, d)
    rw_pad = jnp.zeros((depth, d, LANES), F32).at[:, :, :N_EXPERTS].set(router_w)
    rb_pad = jnp.full((depth, 1, LANES), NEG_INF, F32).at[:, 0, :N_EXPERTS].set(router_b)

    w_in_b, w_gate_b, w_branch_b, w_out_b = (w.astype(BF16) for w in (w_in, w_gate, w_branch, w_out))

    xf = x.reshape(t, d)
    h = None
    for l in range(depth):
        lam_init = 0.8 - 0.6 * math.exp(-0.3 * l)
        sh1, sc1, g1, sh2, sc2, g2 = (mod[l, :, i] for i in range(6))
        if l == 0:
            h = _norm_mod(xf, norm_mix[l].reshape(1, d), sc1, sh1, seq)
        z, qt, vt = _inproj(h, w_in_b, l, rot_c, rot_s1, rot_s2, cos_t, sin_t)
        y_a, w_gu_b, w_dn_b = _attention(z, qt, vt, diff_lambda[l], diff_subln[l].reshape(DA_DV, 1),
                                         expert_w_gu, expert_w_down, l, bsz, seq, lam_init)
        bs_b = jnp.broadcast_to(gmlp_b_spatial[l][:, :, None], (GM_GROUPS, GM_CHUNK, GM_CHUNK))
        y_b = _gmlp(z, gmlp_ln_g[l].reshape(1, MIX_W), gmlp_ln_b[l].reshape(1, MIX_W),
                    gmlp_w_spatial[l], bs_b)
        y_c = _pool(z, pool_w[l].astype(BF16), pool_scale[l].reshape(1, MIX_W), seq)
        y_d = _conv(z, conv_w[l], seq)
        merged = _merge(h, (y_a, y_b, y_c, y_d), w_gate_b, b_gate[l], w_branch_b, l)
        xf, hp, ti, tw, tr, cnt = _outproj(merged, w_out_b, l, xf, g1, norm_ffn[l].reshape(1, d),
                                           sc2, sh2, rw_pad[l], rb_pad[l], seq)
        dest, block_e, n_used, block_valid, runs, n_rows, pad_lo, pad_hi = _routing_tables(
            ti[:, :TOP_K], tr[:, :TOP_K], cnt[0, :N_EXPERTS])
        xb = _dispatch(pad_lo, pad_hi, n_used, dest, hp, n_rows)
        outp = _experts(block_e, n_used, block_valid, *runs, xb, w_gu_b, expert_b_gu[l], w_dn_b,
                        expert_b_down[l])
        if l + 1 < depth:
            nsh1, nsc1 = mod[l + 1, :, 0], mod[l + 1, :, 1]
            xf, h = _combine(dest, tw, xf, g2, norm_mix[l + 1].reshape(1, d), nsc1, nsh1, outp, seq, False)
        else:
            zero = jnp.zeros((bsz, 1, d), F32)
            (out,) = _combine(dest, tw, xf, g2, final_norm.reshape(1, d), zero, zero, outp, seq, True)
    return out.reshape(bsz, seq, d)
```
